```python
import math
import jax, jax.numpy as jnp
from jax import lax
import numpy as np

D_MODEL = 1024
BATCH = 16
SEQ = 2048
DEPTH = 2

N_META = 16
N_EVEN = (DEPTH + 1) // 2
N_ODD = DEPTH // 2

S5_WIDTH = D_MODEL // 2
S5_GROUP = 16
S5_GROUPS = S5_WIDTH // S5_GROUP
S5_STATE = 64
S5_DT_MIN = 0.001
S5_DT_MAX = 0.1

SB_HEAD_DIM = 64
SB_WIDTH = D_MODEL // 2
SB_HEADS = SB_WIDTH // SB_HEAD_DIM
SB_BLOCK = 128

HG_WIDTH = D_MODEL
HG_DK = 128
HG_HEADS = HG_WIDTH // HG_DK
HG_DV = HG_WIDTH // HG_HEADS
HG_CHUNK = 64

D_FF = 4 * D_MODEL
IN_AB = S5_WIDTH + 3 * SB_WIDTH
IN_C = 4 * HG_WIDTH

ALPHA = (2.0 * DEPTH) ** 0.25
BETA = (8.0 * DEPTH) ** -0.25
LN_EPS = 1e-5
RMS_EPS = 1e-6

kernel_name = 'hybrid_s5_stickbreak_hgrn2_deepnorm_meta'


def layer_norm(x, g, b):
    xf = x.astype(jnp.float32)
    mu = jnp.mean(xf, axis=-1, keepdims=True)
    var = jnp.mean(jnp.square(xf - mu), axis=-1, keepdims=True)
    return (xf - mu) * lax.rsqrt(var + LN_EPS) * g + b


def s5_mixer(u, lam_re, lam_im, log_dt, b_re, b_im, c_re, c_im, d, w_glu, b_glu):
    bn, seq_len, _ = u.shape
    f32 = jnp.float32
    uf = u.astype(f32).reshape(bn, seq_len, S5_GROUPS, S5_GROUP)
    lam = lax.complex(lam_re.astype(f32), lam_im.astype(f32))
    dt = jnp.exp(log_dt.astype(f32))[:, None]
    lam_bar = jnp.exp(lam * dt)
    b_mat = lax.complex(b_re.astype(f32), b_im.astype(f32))
    b_bar = ((lam_bar - 1.0) / lam)[:, :, None] * b_mat
    bu = jnp.einsum('blgh,gph->blgp', uf.astype(jnp.complex64), b_bar)
    a = jnp.broadcast_to(lam_bar, (1, seq_len) + lam_bar.shape)

    def combine(left, right):
        a_l, s_l = left
        a_r, s_r = right
        return a_r * a_l, a_r * s_l + s_r

    _, states = lax.associative_scan(combine, (a, bu), axis=1)
    c_mat = lax.complex(c_re.astype(f32), c_im.astype(f32))
    y = jnp.real(jnp.einsum('blgp,ghp->blgh', states, c_mat)) + d.astype(f32) * uf
    y = jax.nn.gelu(y.reshape(bn, seq_len, S5_WIDTH))
    return y * jax.nn.sigmoid(y @ w_glu + b_glu)


def _stick_breaking_block(qb, kb, vb, start):
    z = jnp.einsum('bhqd,bhkd->bhqk', qb, kb)
    nq = qb.shape[2]
    nk = kb.shape[2]
    visible = jnp.arange(nk)[None, :] < (start + jnp.arange(nq))[:, None]
    log_keep = jnp.where(visible, jax.nn.log_sigmoid(-z), 0.0)
    later = lax.cumsum(log_keep, axis=3, reverse=True) - log_keep
    w = jnp.where(visible, jnp.exp(jax.nn.log_sigmoid(z) + later), 0.0)
    return jnp.einsum('bhqk,bhkd->bhqd', w, vb)


def stick_breaking_attention(q, k, v):
    bn, seq_len = q.shape[0], q.shape[1]
    f32 = jnp.float32
    q = q.astype(f32).transpose(0, 2, 1, 3) * (SB_HEAD_DIM ** -0.5)
    k = k.astype(f32).transpose(0, 2, 1, 3)
    v = v.astype(f32).transpose(0, 2, 1, 3)
    blocks = [(0, N_META)] + [(N_META + i * SB_BLOCK, SB_BLOCK)
                              for i in range((seq_len - N_META) // SB_BLOCK)]
    outs = [_stick_breaking_block(q[:, :, s:s + n], k[:, :, :s + n], v[:, :, :s + n], s)
            for s, n in blocks]
    o = jnp.concatenate(outs, axis=2)
    return o.transpose(0, 2, 1, 3).reshape(bn, seq_len, SB_HEADS * SB_HEAD_DIM)


def hgrn_lower_bound(gamma, layer):
    p = jax.nn.softmax(gamma.astype(jnp.float32), axis=0)
    return (jnp.cumsum(p, axis=0) - p[0])[layer]


def _to_chunks(t, n_chunks):
    bn, _, nh, dd = t.shape
    return t.reshape(bn, n_chunks, HG_CHUNK, nh, dd).transpose(1, 0, 3, 2, 4)


def hgrn2_mixer(q, f_pre, i_in, g, lb, norm_g):
    bn, seq_len, _ = q.shape
    f32 = jnp.float32
    f = lb + (1.0 - lb) * jax.nn.sigmoid(f_pre.astype(f32))
    log_f = jnp.log(f)
    k = 1.0 - f
    pad = HG_CHUNK - N_META
    n_chunks = (seq_len + pad) // HG_CHUNK

    def heads(t, dd):
        t = t.astype(f32).reshape(bn, seq_len, HG_HEADS, dd)
        return _to_chunks(jnp.pad(t, ((0, 0), (pad, 0), (0, 0), (0, 0))), n_chunks)

    qh, kh, vh, lh = heads(q, HG_DK), heads(k, HG_DK), heads(i_in, HG_DV), heads(log_f, HG_DK)
    causal = jnp.tril(jnp.ones((HG_CHUNK, HG_CHUNK), dtype=bool))

    def step(state, xs):
        qc, kc, vc, lc = xs
        b = jnp.cumsum(lc, axis=2)
        b_last = b[:, :, -1:, :]
        q_dec = qc * jnp.exp(b)
        scores = jnp.einsum('bhtd,bhsd->bhts', q_dec, kc * jnp.exp(-b))
        scores = jnp.where(causal, scores, 0.0)
        o = (jnp.einsum('bhts,bhse->bhte', scores, vc)
             + jnp.einsum('bhtd,bhde->bhte', q_dec, state))
        state = (jnp.exp(b_last[:, :, 0, :])[..., None] * state
                 + jnp.einsum('bhsd,bhse->bhde', kc * jnp.exp(b_last - b), vc))
        return state, o

    s0 = jnp.zeros((bn, HG_HEADS, HG_DK, HG_DV), f32)
    _, o = lax.scan(step, s0, (qh, kh, vh, lh))
    o = o.transpose(1, 0, 3, 2, 4).reshape(bn, seq_len + pad, HG_HEADS, HG_DV)[:, pad:]
    o = o * lax.rsqrt(jnp.mean(jnp.square(o), axis=-1, keepdims=True) + RMS_EPS)
    o = o * norm_g.astype(f32).reshape(HG_HEADS, HG_DV)
    return o.reshape(bn, seq_len, HG_WIDTH) * jax.nn.silu(g.astype(f32))


def setup_inputs(seed: int = 0) -> dict:
    key = jax.random.key(seed)
    ks = jax.random.split(key, 32)
    f32 = jnp.float32
    nrm = lambda k, shape, scale: jax.random.normal(k, shape, f32) * scale
    n_arange = jnp.pi * jnp.arange(S5_STATE, dtype=f32)
    return {
        'x': nrm(ks[0], (BATCH, SEQ, D_MODEL), 1.0),
        'meta': nrm(ks[1], (N_META, D_MODEL), 1.0),
        'w_in_ab': nrm(ks[2], (N_EVEN, D_MODEL, IN_AB), D_MODEL ** -0.5),
        's5_lam_re': -0.5 * jnp.exp(nrm(ks[3], (N_EVEN, S5_GROUPS, S5_STATE), 0.05)),
        's5_lam_im': n_arange + nrm(ks[4], (N_EVEN, S5_GROUPS, S5_STATE), 0.01),
        's5_log_dt': jax.random.uniform(ks[5], (N_EVEN, S5_GROUPS), f32,
                                        minval=math.log(S5_DT_MIN), maxval=math.log(S5_DT_MAX)),
        's5_b_re': nrm(ks[6], (N_EVEN, S5_GROUPS, S5_STATE, S5_GROUP), (2.0 * S5_GROUP) ** -0.5),
        's5_b_im': nrm(ks[7], (N_EVEN, S5_GROUPS, S5_STATE, S5_GROUP), (2.0 * S5_GROUP) ** -0.5),
        's5_c_re': nrm(ks[8], (N_EVEN, S5_GROUPS, S5_GROUP, S5_STATE), S5_STATE ** -0.5),
        's5_c_im': nrm(ks[9], (N_EVEN, S5_GROUPS, S5_GROUP, S5_STATE), S5_STATE ** -0.5),
        's5_d': nrm(ks[10], (N_EVEN, S5_GROUPS, S5_GROUP), 1.0),
        's5_w_glu': nrm(ks[11], (N_EVEN, S5_WIDTH, S5_WIDTH), S5_WIDTH ** -0.5),
        's5_b_glu': nrm(ks[12], (N_EVEN, S5_WIDTH), 0.01),
        'w_out_ab': nrm(ks[13], (N_EVEN, S5_WIDTH + SB_WIDTH, D_MODEL), BETA * (S5_WIDTH + SB_WIDTH) ** -0.5),
        'w_in_c': nrm(ks[14], (N_ODD, D_MODEL, IN_C), D_MODEL ** -0.5),
        'hgrn_gamma': nrm(ks[15], (DEPTH, HG_WIDTH), 0.1),
        'hgrn_norm_g': 1.0 + nrm(ks[16], (N_ODD, HG_WIDTH), 0.01),
        'w_out_c': nrm(ks[17], (N_ODD, HG_WIDTH, D_MODEL), BETA * HG_WIDTH ** -0.5),
        'ln_mix_g': 1.0 + nrm(ks[18], (DEPTH, D_MODEL), 0.01),
        'ln_mix_b': nrm(ks[19], (DEPTH, D_MODEL), 0.01),
        'mlp_w_up': nrm(ks[20], (DEPTH, D_MODEL, D_FF), D_MODEL ** -0.5),
        'mlp_b_up': nrm(ks[21], (DEPTH, D_FF), 0.01),
        'mlp_w_down': nrm(ks[22], (DEPTH, D_FF, D_MODEL), BETA * D_FF ** -0.5),
        'mlp_b_down': nrm(ks[23], (DEPTH, D_MODEL), 0.01),
        'ln_mlp_g': 1.0 + nrm(ks[24], (DEPTH, D_MODEL), 0.01),
        'ln_mlp_b': nrm(ks[25], (DEPTH, D_MODEL), 0.01),
    }


def reference(x, meta, w_in_ab, s5_lam_re, s5_lam_im, s5_log_dt, s5_b_re, s5_b_im, s5_c_re, s5_c_im,
              s5_d, s5_w_glu, s5_b_glu, w_out_ab, w_in_c, hgrn_gamma, hgrn_norm_g, w_out_c,
              ln_mix_g, ln_mix_b, mlp_w_up, mlp_b_up, mlp_w_down, mlp_b_down, ln_mlp_g, ln_mlp_b):
    bn = x.shape[0]
    h = jnp.concatenate([jnp.broadcast_to(meta.astype(x.dtype)[None], (bn, N_META, D_MODEL)), x], axis=1)
    seq_len = h.shape[1]
    for layer in range(DEPTH):
        if layer % 2 == 0:
            e = layer // 2
            proj = h @ w_in_ab[e]
            u = proj[..., :S5_WIDTH]
            qkv = proj[..., S5_WIDTH:].reshape(bn, seq_len, 3, SB_HEADS, SB_HEAD_DIM)
            a_out = s5_mixer(u, s5_lam_re[e], s5_lam_im[e], s5_log_dt[e], s5_b_re[e], s5_b_im[e],
                             s5_c_re[e], s5_c_im[e], s5_d[e], s5_w_glu[e], s5_b_glu[e])
            b_out = stick_breaking_attention(qkv[:, :, 0], qkv[:, :, 1], qkv[:, :, 2])
            mix = jnp.concatenate([a_out, b_out], axis=-1) @ w_out_ab[e]
        else:
            o_idx = layer // 2
            proj = h @ w_in_c[o_idx]
            q, f_pre, i_in, g = jnp.split(proj, 4, axis=-1)
            lb = hgrn_lower_bound(hgrn_gamma, layer)
            mix = hgrn2_mixer(q, f_pre, i_in, g, lb, hgrn_norm_g[o_idx]) @ w_out_c[o_idx]
        h = layer_norm(ALPHA * h + mix, ln_mix_g[layer], ln_mix_b[layer])
        hid = jnp.square(jax.nn.relu(h @ mlp_w_up[layer] + mlp_b_up[layer]))
        h = layer_norm(ALPHA * h + hid @ mlp_w_down[layer] + mlp_b_down[layer],
                       ln_mlp_g[layer], ln_mlp_b[layer])
    return h[:, N_META:, :]
```

```python
import functools
import math

import jax
import jax.numpy as jnp
from jax import lax
from jax.experimental import pallas as pl
from jax.experimental.pallas import tpu as pltpu

F32 = jnp.float32
BF16 = jnp.bfloat16

N_META = 16
S5_GROUP = 16
S5_STATE = 64
SB_HEAD_DIM = 64
HG_DK = 128
HG_CHUNK = 64
LN_EPS = 1e-5
RMS_EPS = 1e-6

LANES = 128
SEQ_ALIGN = 128
VMEM_LIMIT = 56 * 1024 * 1024


def _row_tile(rows, target):
    best = 8
    for t in range(8, min(rows, target) + 1, 8):
        if rows % t == 0:
            best = t
    return best


def _const_spec(shape):
    nd = len(shape)
    return pl.BlockSpec(shape, lambda *_: (0,) * nd, pipeline_mode=pl.Buffered(1))


def _nt_dot(a, b):
    return lax.dot_general(a, b, (((1,), (1,)), ((), ())), preferred_element_type=F32)


def _split_bf16(x):
    hi = x.astype(BF16)
    lo = (x - hi.astype(F32)).astype(BF16)
    return hi, lo


def _layer_norm(x, g, b):
    mu = jnp.mean(x, axis=-1, keepdims=True)
    xc = x - mu
    var = jnp.mean(xc * xc, axis=-1, keepdims=True)
    return xc * lax.rsqrt(var + LN_EPS) * g + b


def _sigmoid(x):
    return 1.0 / (1.0 + jnp.exp(-x))


def _inproj_ab_kernel(h_ref, w_ref, u_ref, qkv_ref, *, s5w, sbw):
    h = h_ref[...].astype(BF16)
    u_ref[...] = jnp.dot(h, w_ref[:, :s5w], preferred_element_type=F32)
    q = jnp.dot(h, w_ref[:, s5w:s5w + sbw], preferred_element_type=F32)
    qkv_ref[:, :sbw] = (q * (SB_HEAD_DIM ** -0.5)).astype(BF16)
    kv = jnp.dot(h, w_ref[:, s5w + sbw:], preferred_element_type=F32)
    qkv_ref[:, sbw:] = kv.astype(BF16)


def _inproj_ab(h, w, s5w, sbw):
    rows, d = h.shape
    tm = _row_tile(rows, 512)
    return pl.pallas_call(
        functools.partial(_inproj_ab_kernel, s5w=s5w, sbw=sbw),
        grid=(rows // tm,),
        in_specs=[pl.BlockSpec((tm, d), lambda i: (i, 0)), _const_spec(w.shape)],
        out_specs=[pl.BlockSpec((tm, s5w), lambda i: (i, 0)),
                   pl.BlockSpec((tm, 3 * sbw), lambda i: (i, 0))],
        out_shape=[jax.ShapeDtypeStruct((rows, s5w), F32),
                   jax.ShapeDtypeStruct((rows, 3 * sbw), BF16)],
        compiler_params=pltpu.CompilerParams(
            dimension_semantics=("parallel",), vmem_limit_bytes=VMEM_LIMIT),
        name="inproj_ab",
    )(h, w)


def _inproj_c_kernel(h_ref, w_ref, o_ref, *, n_split):
    h = h_ref[...].astype(BF16)
    n = w_ref.shape[1] // n_split
    for s in range(n_split):
        o_ref[:, s * n:(s + 1) * n] = jnp.dot(
            h, w_ref[:, s * n:(s + 1) * n], preferred_element_type=F32)


def _inproj_c(h, w):
    rows, d = h.shape
    n = w.shape[1]
    tm = _row_tile(rows, 512)
    return pl.pallas_call(
        functools.partial(_inproj_c_kernel, n_split=4),
        grid=(rows // tm,),
        in_specs=[pl.BlockSpec((tm, d), lambda i: (i, 0)), _const_spec(w.shape)],
        out_specs=pl.BlockSpec((tm, n), lambda i: (i, 0)),
        out_shape=jax.ShapeDtypeStruct((rows, n), F32),
        compiler_params=pltpu.CompilerParams(
            dimension_semantics=("parallel",), vmem_limit_bytes=VMEM_LIMIT),
        name="inproj_c",
    )(h, w)


def _s5_kernel(u_ref, wb_ref, are_ref, aim_ref, wc_ref, d_ref, wglu_ref, bglu_ref, o_ref,
               bu_ref, sre_ref, sim_ref, y_ref, *, nb, tc, slab_group):
    width = u_ref.shape[2]
    sc = are_ref.shape[1]
    nq = width // LANES
    per_q = sc // nq

    @pl.when(pl.program_id(0) == 0)
    def _():
        sre_ref[...] = jnp.zeros_like(sre_ref)
        sim_ref[...] = jnp.zeros_like(sim_ref)

    n_slabs = sc // LANES
    slabs_q = per_q // LANES
    u = u_ref[...].reshape(nb * tc, width)
    ub = u.astype(BF16)
    for q in range(nq):
        r = jnp.dot(ub[:, q * LANES:(q + 1) * LANES], wb_ref[q], preferred_element_type=F32)
        for j in range(slabs_q):
            bu_ref[q * slabs_q + j] = r[:, j * LANES:(j + 1) * LANES]
            bu_ref[n_slabs + q * slabs_q + j] = r[:, per_q + j * LANES:per_q + (j + 1) * LANES]

    for s0 in range(0, n_slabs, slab_group):
        slabs = list(range(s0, s0 + slab_group))
        a_re = [jnp.broadcast_to(are_ref[:, s * LANES:(s + 1) * LANES], (nb, LANES)) for s in slabs]
        a_im = [jnp.broadcast_to(aim_ref[:, s * LANES:(s + 1) * LANES], (nb, LANES)) for s in slabs]
        init = []
        for s in slabs:
            init += [sre_ref[:, s * LANES:(s + 1) * LANES], sim_ref[:, s * LANES:(s + 1) * LANES]]

        def body(t, carry, slabs=slabs, a_re=a_re, a_im=a_im):
            new = []
            rows = pl.ds(t, nb, stride=tc)
            for i, s in enumerate(slabs):
                sr, si = carry[2 * i], carry[2 * i + 1]
                nr = a_re[i] * sr - a_im[i] * si + bu_ref[s, rows, :]
                ni = a_re[i] * si + a_im[i] * sr + bu_ref[n_slabs + s, rows, :]
                bu_ref[s, rows, :] = nr
                bu_ref[n_slabs + s, rows, :] = ni
                new += [nr, ni]
            return tuple(new)

        fin = lax.fori_loop(0, tc, body, tuple(init))
        for i, s in enumerate(slabs):
            sre_ref[:, s * LANES:(s + 1) * LANES] = fin[2 * i]
            sim_ref[:, s * LANES:(s + 1) * LANES] = fin[2 * i + 1]

    for q in range(nq):
        s_re = jnp.concatenate(
            [bu_ref[q * slabs_q + j].astype(BF16) for j in range(slabs_q)], axis=1)
        s_im = jnp.concatenate(
            [bu_ref[n_slabs + q * slabs_q + j].astype(BF16) for j in range(slabs_q)], axis=1)
        y = (jnp.dot(s_re, wc_ref[q, :per_q, :], preferred_element_type=F32)
             + jnp.dot(s_im, wc_ref[q, per_q:, :], preferred_element_type=F32))
        y = y + d_ref[:, q * LANES:(q + 1) * LANES] * u[:, q * LANES:(q + 1) * LANES]
        y = 0.5 * y * (1.0 + jnp.tanh(math.sqrt(2.0 / math.pi) * (y + 0.044715 * (y * y * y))))
        y_ref[:, q * LANES:(q + 1) * LANES] = y
    y = y_ref[...]
    gate = jnp.dot(y.astype(BF16), wglu_ref[...], preferred_element_type=F32) + bglu_ref[...]
    o_ref[...] = (y * _sigmoid(gate)).reshape(nb, tc, width).astype(o_ref.dtype)


def _s5_params(lam_re, lam_im, log_dt, b_re, b_im, c_re, c_im):
    g, p = lam_re.shape
    lam = lax.complex(lam_re.astype(F32), lam_im.astype(F32))
    dt = jnp.exp(log_dt.astype(F32))[:, None]
    lam_bar = jnp.exp(lam * dt)
    b_bar = ((lam_bar - 1.0) / lam)[:, :, None] * lax.complex(b_re.astype(F32), b_im.astype(F32))
    gq = LANES // S5_GROUP
    nq = g // gq
    eye = jnp.eye(gq, dtype=F32)

    def b_slab(x):
        x = x.reshape(nq, gq, p, S5_GROUP)
        return jnp.einsum('qgph,gk->qghkp', x, eye).reshape(nq, gq * S5_GROUP, gq * p)

    def c_slab(x):
        x = x.reshape(nq, gq, S5_GROUP, p)
        return jnp.einsum('qghp,gk->qgpkh', x, eye).reshape(nq, gq * p, gq * S5_GROUP)

    wb = jnp.concatenate([b_slab(jnp.real(b_bar)), b_slab(jnp.imag(b_bar))], axis=2)
    wc = jnp.concatenate([c_slab(c_re.astype(F32)), c_slab(-c_im.astype(F32))], axis=1)
    a_re = jnp.real(lam_bar).reshape(1, g * p)
    a_im = jnp.imag(lam_bar).reshape(1, g * p)
    return wb.astype(BF16), wc.astype(BF16), a_re, a_im


def _s5(u3, wb, a_re, a_im, wc, d, wglu, bglu):
    nb, lp, width = u3.shape
    sc = a_re.shape[1]
    tc = 64
    assert lp % tc == 0
    return pl.pallas_call(
        functools.partial(_s5_kernel, nb=nb, tc=tc, slab_group=4),
        grid=(lp // tc,),
        in_specs=[pl.BlockSpec((nb, tc, width), lambda t: (0, t, 0)),
                  _const_spec(wb.shape), _const_spec(a_re.shape), _const_spec(a_im.shape),
                  _const_spec(wc.shape), _const_spec(d.shape), _const_spec(wglu.shape),
                  _const_spec(bglu.shape)],
        out_specs=pl.BlockSpec((nb, tc, width), lambda t: (0, t, 0)),
        out_shape=jax.ShapeDtypeStruct((nb, lp, width), BF16),
        scratch_shapes=[pltpu.VMEM((2 * sc // LANES, nb * tc, LANES), F32),
                        pltpu.VMEM((nb, sc), F32), pltpu.VMEM((nb, sc), F32),
                        pltpu.VMEM((nb * tc, width), F32)],
        compiler_params=pltpu.CompilerParams(
            dimension_semantics=("arbitrary",), vmem_limit_bytes=VMEM_LIMIT),
        name="s5_mixer",
    )(u3, wb, a_re, a_im, wc, d, wglu, bglu)


def _sb_kernel(q_ref, k_ref, v_ref, o_ref, *, tq):
    qi = pl.program_id(2)
    lane = lax.broadcasted_iota(jnp.int32, (tq, LANES), 1)
    row = lax.broadcasted_iota(jnp.int32, (tq, tq), 0)
    col = lax.broadcasted_iota(jnp.int32, (tq, tq), 1)
    visible = col < row
    tri = jnp.where(row > col, 1.0, 0.0).astype(BF16)
    tri_ones = jnp.concatenate([tri, jnp.ones((tq, tq), BF16)], axis=1)
    q = q_ref[0]
    zero = jnp.zeros_like(q)
    accs = []
    for h in range(LANES // SB_HEAD_DIM):
        in_head = (lane >= h * SB_HEAD_DIM) & (lane < (h + 1) * SB_HEAD_DIM)
        qh = jnp.where(in_head, q, zero)

        def tile(j, carry, acc, diagonal, qh=qh):
            r0 = pl.multiple_of(j * tq, tq)
            z = _nt_dot(qh, k_ref[0, pl.ds(r0, tq), :])
            lk = -(jnp.maximum(z, 0.0) + jnp.log(1.0 + jnp.exp(-jnp.abs(z))))
            lkm = jnp.where(visible, lk, 0.0) if diagonal else lk
            hi, lo = _split_bf16(lkm)
            cs = (jnp.dot(hi, tri_ones, preferred_element_type=F32)
                  + jnp.dot(lo, tri_ones, preferred_element_type=F32))
            w = jnp.exp(z + lk + cs[:, :tq] + carry)
            if diagonal:
                w = jnp.where(visible, w, 0.0)
            acc = acc + jnp.dot(w.astype(BF16), v_ref[0, pl.ds(r0, tq), :],
                                preferred_element_type=F32)
            return carry + cs[:, tq:], acc

        carry, acc = tile(qi, jnp.zeros((tq, tq), F32), jnp.zeros((tq, LANES), F32), True)

        def body(i, ca):
            return tile(qi - 1 - i, ca[0], ca[1], False)

        carry, acc = lax.fori_loop(0, qi, body, (carry, acc))
        accs.append(jnp.where(in_head, acc, 0.0))
    o_ref[0] = (accs[0] + accs[1]).astype(o_ref.dtype)


def _stick_breaking(qkv3, sbw):
    nb, lp, _ = qkv3.shape
    tq = SEQ_ALIGN
    n_pairs = sbw // LANES
    return pl.pallas_call(
        functools.partial(_sb_kernel, tq=tq),
        grid=(nb, n_pairs, lp // tq),
        in_specs=[pl.BlockSpec((1, tq, LANES), lambda b, p, i: (b, i, p)),
                  pl.BlockSpec((1, lp, LANES), lambda b, p, i: (b, 0, n_pairs + p)),
                  pl.BlockSpec((1, lp, LANES), lambda b, p, i: (b, 0, 2 * n_pairs + p))],
        out_specs=pl.BlockSpec((1, tq, LANES), lambda b, p, i: (b, i, p)),
        out_shape=jax.ShapeDtypeStruct((nb, lp, sbw), BF16),
        compiler_params=pltpu.CompilerParams(
            dimension_semantics=("parallel", "parallel", "arbitrary"),
            vmem_limit_bytes=VMEM_LIMIT),
        name="stick_breaking",
    )(qkv3, qkv3, qkv3)


def _hgrn_kernel(q_ref, f_ref, v_ref, g_ref, lb_ref, ng_ref, o_ref, *, heads, chunk):
    lp = q_ref.shape[1]
    row = lax.broadcasted_iota(jnp.int32, (chunk, chunk), 0)
    col = lax.broadcasted_iota(jnp.int32, (chunk, chunk), 1)
    causal = col <= row
    tri = jnp.where(causal, 1.0, 0.0).astype(BF16)

    def body(c, states):
        r0 = pl.multiple_of(c * chunk, chunk)
        new_states = []
        for h in range(heads):
            cs = slice(h * HG_DK, (h + 1) * HG_DK)
            lb = lb_ref[:, cs]
            f = lb + (1.0 - lb) * _sigmoid(f_ref[0, pl.ds(r0, chunk), cs])
            lc = jnp.log(f)
            kk = 1.0 - f
            hi, lo = _split_bf16(lc)
            bcum = (jnp.dot(tri, hi, preferred_element_type=F32)
                    + jnp.dot(tri, lo, preferred_element_type=F32))
            b_last = bcum[chunk - 1:chunk, :]
            q_dec = (q_ref[0, pl.ds(r0, chunk), cs] * jnp.exp(bcum)).astype(BF16)
            k_inv = (kk * jnp.exp(-bcum)).astype(BF16)
            k_end = (kk * jnp.exp(b_last - bcum)).astype(BF16)
            v = v_ref[0, pl.ds(r0, chunk), cs].astype(BF16)
            scores = jnp.where(causal, _nt_dot(q_dec, k_inv), 0.0)
            st = states[h]
            o = (jnp.dot(scores.astype(BF16), v, preferred_element_type=F32)
                 + _nt_dot(q_dec, st.astype(BF16)))
            new_states.append(
                st * jnp.exp(b_last)
                + lax.dot_general(v, k_end, (((0,), (0,)), ((), ())), preferred_element_type=F32))
            o = o * lax.rsqrt(jnp.mean(o * o, axis=-1, keepdims=True) + RMS_EPS)
            g = g_ref[0, pl.ds(r0, chunk), cs]
            o = o * ng_ref[:, cs] * (g * _sigmoid(g))
            o_ref[0, pl.ds(r0, chunk), cs] = o.astype(o_ref.dtype)
        return tuple(new_states)

    init = tuple(jnp.zeros((HG_DK, HG_DK), F32) for _ in range(heads))
    lax.fori_loop(0, lp // chunk, body, init)


def _hgrn2(proj3, lb, norm_g, width):
    nb, lp, _ = proj3.shape
    heads = 2
    bw = heads * HG_DK
    n_blk = width // bw

    def col_spec(part):
        return pl.BlockSpec((1, lp, bw), lambda b, j, part=part: (b, 0, part * n_blk + j))

    vec_spec = pl.BlockSpec((1, bw), lambda b, j: (0, j))
    return pl.pallas_call(
        functools.partial(_hgrn_kernel, heads=heads, chunk=HG_CHUNK),
        grid=(nb, n_blk),
        in_specs=[col_spec(0), col_spec(1), col_spec(2), col_spec(3), vec_spec, vec_spec],
        out_specs=pl.BlockSpec((1, lp, bw), lambda b, j: (b, 0, j)),
        out_shape=jax.ShapeDtypeStruct((nb, lp, width), BF16),
        compiler_params=pltpu.CompilerParams(
            dimension_semantics=("parallel", "parallel"), vmem_limit_bytes=VMEM_LIMIT),
        name="hgrn2_mixer",
    )(proj3, proj3, proj3, proj3, lb, norm_g)


def _block_tail_kernel(*refs, n_mix, alpha, ff_chunks):
    h_ref = refs[0]
    mix_refs = refs[1:1 + n_mix]
    w_refs = refs[1 + n_mix:1 + 2 * n_mix]
    (g1_ref, b1_ref, wup_ref, bup_ref, wdn_ref, bdn_ref, g2_ref, b2_ref, o_ref) = refs[1 + 2 * n_mix:]
    mix = jnp.dot(mix_refs[0][...], w_refs[0][...], preferred_element_type=F32)
    for m_ref, w_ref in zip(mix_refs[1:], w_refs[1:]):
        mix = mix + jnp.dot(m_ref[...], w_ref[...], preferred_element_type=F32)
    h1 = _layer_norm(alpha * h_ref[...] + mix, g1_ref[...], b1_ref[...])
    h1b = h1.astype(BF16)
    dff = wup_ref.shape[1]
    fc = dff // ff_chunks
    acc = alpha * h1 + bdn_ref[...]
    for c in range(ff_chunks):
        hid = jnp.dot(h1b, wup_ref[:, c * fc:(c + 1) * fc], preferred_element_type=F32)
        hid = jnp.maximum(hid + bup_ref[:, c * fc:(c + 1) * fc], 0.0)
        hid = (hid * hid).astype(BF16)
        acc = acc + jnp.dot(hid, wdn_ref[c * fc:(c + 1) * fc, :], preferred_element_type=F32)
    o_ref[...] = _layer_norm(acc, g2_ref[...], b2_ref[...])


def _block_tail(h, mixes, ws, g1, b1, wup, bup, wdn, bdn, g2, b2, alpha):
    rows, d = h.shape
    tm = _row_tile(rows, 512)
    row_spec = lambda n: pl.BlockSpec((tm, n), lambda i: (i, 0))
    consts = [g1, b1, wup, bup, wdn, bdn, g2, b2]
    return pl.pallas_call(
        functools.partial(_block_tail_kernel, n_mix=len(mixes), alpha=alpha, ff_chunks=4),
        grid=(rows // tm,),
        in_specs=([row_spec(d)] + [row_spec(m.shape[1]) for m in mixes]
                  + [_const_spec(w.shape) for w in ws] + [_const_spec(c.shape) for c in consts]),
        out_specs=row_spec(d),
        out_shape=jax.ShapeDtypeStruct((rows, d), F32),
        compiler_params=pltpu.CompilerParams(
            dimension_semantics=("parallel",), vmem_limit_bytes=VMEM_LIMIT),
        name="block_tail",
    )(h, *mixes, *ws, *consts)


def kernel(x, meta, w_in_ab, s5_lam_re, s5_lam_im, s5_log_dt, s5_b_re, s5_b_im, s5_c_re, s5_c_im, s5_d, s5_w_glu, s5_b_glu, w_out_ab, w_in_c, hgrn_gamma, hgrn_norm_g, w_out_c, ln_mix_g, ln_mix_b, mlp_w_up, mlp_b_up, mlp_w_down, mlp_b_down, ln_mlp_g, ln_mlp_b):
    nb, seq, d = x.shape
    depth = ln_mix_g.shape[0]
    assert depth == 2 and w_in_ab.shape[0] == 1 and w_in_c.shape[0] == 1
    alpha = (2.0 * depth) ** 0.25
    s5w = s5_w_glu.shape[1]
    sbw = (w_in_ab.shape[2] - s5w) // 3
    hgw = w_in_c.shape[2] // 4
    ltok = N_META + seq
    lp = -(-ltok // SEQ_ALIGN) * SEQ_ALIGN
    rows = nb * lp
    row2 = lambda a: a.reshape(1, -1).astype(F32)

    h = jnp.concatenate([jnp.broadcast_to(meta.astype(x.dtype)[None], (nb, N_META, d)), x,
                         jnp.zeros((nb, lp - ltok, d), x.dtype)], axis=1).reshape(rows, d)

    u, qkv = _inproj_ab(h, w_in_ab[0].astype(BF16), s5w, sbw)
    wb, wc, a_re, a_im = _s5_params(s5_lam_re[0], s5_lam_im[0], s5_log_dt[0], s5_b_re[0],
                                    s5_b_im[0], s5_c_re[0], s5_c_im[0])
    a_out = _s5(u.reshape(nb, lp, s5w), wb, a_re, a_im, wc, row2(s5_d[0]),
                s5_w_glu[0].astype(BF16), row2(s5_b_glu[0]))
    b_out = _stick_breaking(qkv.reshape(nb, lp, 3 * sbw), sbw)
    w_out = w_out_ab[0].astype(BF16)
    h = _block_tail(h, [a_out.reshape(rows, s5w), b_out.reshape(rows, sbw)],
                    [w_out[:s5w], w_out[s5w:]],
                    row2(ln_mix_g[0]), row2(ln_mix_b[0]), mlp_w_up[0].astype(BF16),
                    row2(mlp_b_up[0]), mlp_w_down[0].astype(BF16), row2(mlp_b_down[0]),
                    row2(ln_mlp_g[0]), row2(ln_mlp_b[0]), alpha)

    proj = _inproj_c(h, w_in_c[0].astype(BF16))
    p = jax.nn.softmax(hgrn_gamma.astype(F32), axis=0)
    lb = (jnp.cumsum(p, axis=0) - p[0])[1]
    c_out = _hgrn2(proj.reshape(nb, lp, 4 * hgw), row2(lb), row2(hgrn_norm_g[0]), hgw)
    h = _block_tail(h, [c_out.reshape(rows, hgw)], [w_out_c[0].astype(BF16)],
                    row2(ln_mix_g[1]), row2(ln_mix_b[1]), mlp_w_up[1].astype(BF16),
                    row2(mlp_b_up[1]), mlp_w_down[1].astype(BF16), row2(mlp_b_down[1]),
                    row2(ln_mlp_g[1]), row2(ln_mlp_b[1]), alpha)
    return h.reshape(nb, lp, d)[:, N_META:ltok, :]
```

```python
import functools
import math

import jax
import jax.numpy as jnp
from jax import lax
from jax.experimental import pallas as pl
from jax.experimental.pallas import tpu as pltpu

F32 = jnp.float32
BF16 = jnp.bfloat16

N_META = 16
S5_GROUP = 16
S5_STATE = 64
SB_HEAD_DIM = 64
HG_DK = 128
HG_CHUNK = 64
LN_EPS = 1e-5
RMS_EPS = 1e-6

LANES = 128
SEQ_ALIGN = 128
VMEM_LIMIT = 56 * 1024 * 1024


def _row_tile(rows, target):
    best = 8
    for t in range(8, min(rows, target) + 1, 8):
        if rows % t == 0:
            best = t
    return best


def _const_spec(shape):
    nd = len(shape)
    return pl.BlockSpec(shape, lambda *_: (0,) * nd, pipeline_mode=pl.Buffered(1))


def _nt_dot(a, b):
    return lax.dot_general(a, b, (((1,), (1,)), ((), ())), preferred_element_type=F32)


def _split_bf16(x):
    hi = x.astype(BF16)
    lo = (x - hi.astype(F32)).astype(BF16)
    return hi, lo


def _layer_norm(x, g, b):
    mu = jnp.mean(x, axis=-1, keepdims=True)
    xc = x - mu
    var = jnp.mean(xc * xc, axis=-1, keepdims=True)
    return xc * lax.rsqrt(var + LN_EPS) * g + b


def _sigmoid(x):
    return 1.0 / (1.0 + jnp.exp(-x))


def _inproj_ab_kernel(h_ref, w_ref, u_ref, qkv_ref, *, s5w, sbw):
    h = h_ref[...].astype(BF16)
    u_ref[...] = jnp.dot(h, w_ref[:, :s5w], preferred_element_type=F32)
    q = jnp.dot(h, w_ref[:, s5w:s5w + sbw], preferred_element_type=F32)
    qkv_ref[:, :sbw] = (q * (SB_HEAD_DIM ** -0.5)).astype(BF16)
    kv = jnp.dot(h, w_ref[:, s5w + sbw:], preferred_element_type=F32)
    qkv_ref[:, sbw:] = kv.astype(BF16)


def _inproj_ab(h, w, s5w, sbw):
    rows, d = h.shape
    tm = _row_tile(rows, 512)
    return pl.pallas_call(
        functools.partial(_inproj_ab_kernel, s5w=s5w, sbw=sbw),
        grid=(rows // tm,),
        in_specs=[pl.BlockSpec((tm, d), lambda i: (i, 0)), _const_spec(w.shape)],
        out_specs=[pl.BlockSpec((tm, s5w), lambda i: (i, 0)),
                   pl.BlockSpec((tm, 3 * sbw), lambda i: (i, 0))],
        out_shape=[jax.ShapeDtypeStruct((rows, s5w), F32),
                   jax.ShapeDtypeStruct((rows, 3 * sbw), BF16)],
        compiler_params=pltpu.CompilerParams(
            dimension_semantics=("parallel",), vmem_limit_bytes=VMEM_LIMIT),
        name="inproj_ab",
    )(h, w)


def _inproj_c_kernel(h_ref, w_ref, o_ref, *, n_split):
    h = h_ref[...].astype(BF16)
    n = w_ref.shape[1] // n_split
    for s in range(n_split):
        o_ref[:, s * n:(s + 1) * n] = jnp.dot(
            h, w_ref[:, s * n:(s + 1) * n], preferred_element_type=F32)


def _inproj_c(h, w):
    rows, d = h.shape
    n = w.shape[1]
    tm = _row_tile(rows, 512)
    return pl.pallas_call(
        functools.partial(_inproj_c_kernel, n_split=4),
        grid=(rows // tm,),
        in_specs=[pl.BlockSpec((tm, d), lambda i: (i, 0)), _const_spec(w.shape)],
        out_specs=pl.BlockSpec((tm, n), lambda i: (i, 0)),
        out_shape=jax.ShapeDtypeStruct((rows, n), F32),
        compiler_params=pltpu.CompilerParams(
            dimension_semantics=("parallel",), vmem_limit_bytes=VMEM_LIMIT),
        name="inproj_c",
    )(h, w)


def _s5_kernel(u_ref, wb_ref, are_ref, aim_ref, wc_ref, d_ref, wglu_ref, bglu_ref, o_ref,
               bu_ref, sre_ref, sim_ref, y_ref, *, nb, tc, slab_group):
    width = u_ref.shape[2]
    sc = are_ref.shape[1]
    nq = width // LANES
    per_q = sc // nq

    @pl.when(pl.program_id(0) == 0)
    def _():
        sre_ref[...] = jnp.zeros_like(sre_ref)
        sim_ref[...] = jnp.zeros_like(sim_ref)

    n_slabs = sc // LANES
    slabs_q = per_q // LANES
    u = u_ref[...].reshape(nb * tc, width)
    ub = u.astype(BF16)
    for q in range(nq):
        r = jnp.dot(ub[:, q * LANES:(q + 1) * LANES], wb_ref[q], preferred_element_type=F32)
        for j in range(slabs_q):
            bu_ref[q * slabs_q + j] = r[:, j * LANES:(j + 1) * LANES]
            bu_ref[n_slabs + q * slabs_q + j] = r[:, per_q + j * LANES:per_q + (j + 1) * LANES]

    for s0 in range(0, n_slabs, slab_group):
        slabs = list(range(s0, s0 + slab_group))
        a_re = [jnp.broadcast_to(are_ref[:, s * LANES:(s + 1) * LANES], (nb, LANES)) for s in slabs]
        a_im = [jnp.broadcast_to(aim_ref[:, s * LANES:(s + 1) * LANES], (nb, LANES)) for s in slabs]
        init = []
        for s in slabs:
            init += [sre_ref[:, s * LANES:(s + 1) * LANES], sim_ref[:, s * LANES:(s + 1) * LANES]]

        def body(t, carry, slabs=slabs, a_re=a_re, a_im=a_im):
            new = []
            rows = pl.ds(t, nb, stride=tc)
            for i, s in enumerate(slabs):
                sr, si = carry[2 * i], carry[2 * i + 1]
                nr = a_re[i] * sr - a_im[i] * si + bu_ref[s, rows, :]
                ni = a_re[i] * si + a_im[i] * sr + bu_ref[n_slabs + s, rows, :]
                bu_ref[s, rows, :] = nr
                bu_ref[n_slabs + s, rows, :] = ni
                new += [nr, ni]
            return tuple(new)

        fin = lax.fori_loop(0, tc, body, tuple(init))
        for i, s in enumerate(slabs):
            sre_ref[:, s * LANES:(s + 1) * LANES] = fin[2 * i]
            sim_ref[:, s * LANES:(s + 1) * LANES] = fin[2 * i + 1]

    for q in range(nq):
        s_re = jnp.concatenate(
            [bu_ref[q * slabs_q + j].astype(BF16) for j in range(slabs_q)], axis=1)
        s_im = jnp.concatenate(
            [bu_ref[n_slabs + q * slabs_q + j].astype(BF16) for j in range(slabs_q)], axis=1)
        y = (jnp.dot(s_re, wc_ref[q, :per_q, :], preferred_element_type=F32)
             + jnp.dot(s_im, wc_ref[q, per_q:, :], preferred_element_type=F32))
        y = y + d_ref[:, q * LANES:(q + 1) * LANES] * u[:, q * LANES:(q + 1) * LANES]
        y = 0.5 * y * (1.0 + jnp.tanh(math.sqrt(2.0 / math.pi) * (y + 0.044715 * (y * y * y))))
        y_ref[:, q * LANES:(q + 1) * LANES] = y
    y = y_ref[...]
    gate = jnp.dot(y.astype(BF16), wglu_ref[...], preferred_element_type=F32) + bglu_ref[...]
    o_ref[...] = (y * _sigmoid(gate)).reshape(nb, tc, width).astype(o_ref.dtype)


def _s5_params(lam_re, lam_im, log_dt, b_re, b_im, c_re, c_im):
    g, p = lam_re.shape
    lr, li = lam_re.astype(F32), lam_im.astype(F32)
    dt = jnp.exp(log_dt.astype(F32))[:, None]
    mag = jnp.exp(lr * dt)
    a_re = mag * jnp.cos(li * dt)
    a_im = mag * jnp.sin(li * dt)
    den = lr * lr + li * li
    c_re_ = ((a_re - 1.0) * lr + a_im * li) / den
    c_im_ = (a_im * lr - (a_re - 1.0) * li) / den
    bb_re = c_re_[:, :, None] * b_re.astype(F32) - c_im_[:, :, None] * b_im.astype(F32)
    bb_im = c_re_[:, :, None] * b_im.astype(F32) + c_im_[:, :, None] * b_re.astype(F32)
    gq = LANES // S5_GROUP
    nq = g // gq
    eye = jnp.eye(gq, dtype=F32)

    def b_slab(x):
        x = x.reshape(nq, gq, p, S5_GROUP)
        return jnp.einsum('qgph,gk->qghkp', x, eye).reshape(nq, gq * S5_GROUP, gq * p)

    def c_slab(x):
        x = x.reshape(nq, gq, S5_GROUP, p)
        return jnp.einsum('qghp,gk->qgpkh', x, eye).reshape(nq, gq * p, gq * S5_GROUP)

    wb = jnp.concatenate([b_slab(bb_re), b_slab(bb_im)], axis=2)
    wc = jnp.concatenate([c_slab(c_re.astype(F32)), c_slab(-c_im.astype(F32))], axis=1)
    return wb.astype(BF16), wc.astype(BF16), a_re.reshape(1, g * p), a_im.reshape(1, g * p)


def _s5(u3, wb, a_re, a_im, wc, d, wglu, bglu):
    nb, lp, width = u3.shape
    sc = a_re.shape[1]
    tc = 64
    assert lp % tc == 0
    return pl.pallas_call(
        functools.partial(_s5_kernel, nb=nb, tc=tc, slab_group=4),
        grid=(lp // tc,),
        in_specs=[pl.BlockSpec((nb, tc, width), lambda t: (0, t, 0)),
                  _const_spec(wb.shape), _const_spec(a_re.shape), _const_spec(a_im.shape),
                  _const_spec(wc.shape), _const_spec(d.shape), _const_spec(wglu.shape),
                  _const_spec(bglu.shape)],
        out_specs=pl.BlockSpec((nb, tc, width), lambda t: (0, t, 0)),
        out_shape=jax.ShapeDtypeStruct((nb, lp, width), BF16),
        scratch_shapes=[pltpu.VMEM((2 * sc // LANES, nb * tc, LANES), F32),
                        pltpu.VMEM((nb, sc), F32), pltpu.VMEM((nb, sc), F32),
                        pltpu.VMEM((nb * tc, width), F32)],
        compiler_params=pltpu.CompilerParams(
            dimension_semantics=("arbitrary",), vmem_limit_bytes=VMEM_LIMIT),
        name="s5_mixer",
    )(u3, wb, a_re, a_im, wc, d, wglu, bglu)


def _sb_kernel(q_ref, k_ref, v_ref, o_ref, km_ref, vm_ref, *, tq, kb):
    qi = pl.program_id(2)
    lp = k_ref.shape[1]
    n_blocks = km_ref.shape[0]
    n_heads = LANES // SB_HEAD_DIM

    @pl.when(qi == 0)
    def _():
        lane = lax.broadcasted_iota(jnp.int32, (kb, LANES), 1)
        for m in range(n_blocks):
            valid = min(kb, lp - m * kb)
            k_blk = k_ref[0, m * kb:m * kb + valid, :]
            v_blk = v_ref[0, m * kb:m * kb + valid, :]
            if valid < kb:
                pad = jnp.zeros((kb - valid, LANES), k_blk.dtype)
                k_blk = jnp.concatenate([k_blk, pad], axis=0)
                v_blk = jnp.concatenate([v_blk, pad], axis=0)
            for h in range(n_heads):
                in_head = (lane >= h * SB_HEAD_DIM) & (lane < (h + 1) * SB_HEAD_DIM)
                km_ref[m, h] = jnp.where(in_head, k_blk, jnp.zeros_like(k_blk))
                vm_ref[m, h] = jnp.where(in_head, v_blk, jnp.zeros_like(v_blk))

    row = lax.broadcasted_iota(jnp.int32, (kb, kb), 0)
    col = lax.broadcasted_iota(jnp.int32, (kb, kb), 1)
    tri = jnp.where(row > col, 1.0, 0.0).astype(BF16)
    q = q_ref[0]

    def block(m, carry, acc, visible):
        new_carry = []
        for h in range(n_heads):
            z = _nt_dot(q, km_ref[m, h])
            soft = jnp.log(1.0 + jnp.exp(-jnp.abs(z)))
            lk = jnp.minimum(-z, 0.0) - soft
            if visible is not None:
                lk = jnp.where(visible, lk, 0.0)
            hi, lo = _split_bf16(lk)
            later = (jnp.dot(hi, tri, preferred_element_type=F32)
                     + jnp.dot(lo, tri, preferred_element_type=F32))
            c = jnp.concatenate([carry[h]] * (kb // LANES), axis=1)
            w = jnp.exp(jnp.minimum(z, 0.0) - soft + later + c)
            if visible is not None:
                w = jnp.where(visible, w, 0.0)
            acc = acc + jnp.dot(w.astype(BF16), vm_ref[m, h], preferred_element_type=F32)
            new_carry.append(carry[h] + jnp.sum(lk, axis=1, keepdims=True))
        return tuple(new_carry), acc

    m_diag = (qi * tq) // kb
    q_pos = qi * tq + lax.broadcasted_iota(jnp.int32, (tq, kb), 0)
    k_pos = m_diag * kb + lax.broadcasted_iota(jnp.int32, (tq, kb), 1)
    carry = tuple(jnp.zeros((tq, LANES), F32) for _ in range(n_heads))
    carry, acc = block(m_diag, carry, jnp.zeros((tq, LANES), F32), k_pos < q_pos)

    def body(i, ca):
        return block(m_diag - 1 - i, ca[0], ca[1], None)

    carry, acc = lax.fori_loop(0, m_diag, body, (carry, acc))
    o_ref[0] = acc.astype(o_ref.dtype)


def _stick_breaking(qkv3, sbw):
    nb, lp, _ = qkv3.shape
    tq = SEQ_ALIGN
    kb = 2 * SEQ_ALIGN
    n_pairs = sbw // LANES
    n_blocks = -(-lp // kb)
    masked_kv = pltpu.VMEM((n_blocks, LANES // SB_HEAD_DIM, kb, LANES), BF16)
    return pl.pallas_call(
        functools.partial(_sb_kernel, tq=tq, kb=kb),
        grid=(nb, n_pairs, lp // tq),
        in_specs=[pl.BlockSpec((1, tq, LANES), lambda b, p, i: (b, i, p)),
                  pl.BlockSpec((1, lp, LANES), lambda b, p, i: (b, 0, n_pairs + p)),
                  pl.BlockSpec((1, lp, LANES), lambda b, p, i: (b, 0, 2 * n_pairs + p))],
        out_specs=pl.BlockSpec((1, tq, LANES), lambda b, p, i: (b, i, p)),
        out_shape=jax.ShapeDtypeStruct((nb, lp, sbw), BF16),
        scratch_shapes=[masked_kv, masked_kv],
        compiler_params=pltpu.CompilerParams(
            dimension_semantics=("parallel", "parallel", "arbitrary"),
            vmem_limit_bytes=VMEM_LIMIT),
        name="stick_breaking",
    )(qkv3, qkv3, qkv3)


def _hgrn_kernel(q_ref, f_ref, v_ref, g_ref, lb_ref, ng_ref, o_ref, *, heads, chunk):
    lp = q_ref.shape[1]
    row = lax.broadcasted_iota(jnp.int32, (chunk, chunk), 0)
    col = lax.broadcasted_iota(jnp.int32, (chunk, chunk), 1)
    causal = col <= row
    tri = jnp.where(causal, 1.0, 0.0).astype(BF16)

    def body(c, states):
        r0 = pl.multiple_of(c * chunk, chunk)
        new_states = []
        for h in range(heads):
            cs = slice(h * HG_DK, (h + 1) * HG_DK)
            lb = lb_ref[:, cs]
            f = lb + (1.0 - lb) * _sigmoid(f_ref[0, pl.ds(r0, chunk), cs])
            lc = jnp.log(f)
            kk = 1.0 - f
            hi, lo = _split_bf16(lc)
            bcum = (jnp.dot(tri, hi, preferred_element_type=F32)
                    + jnp.dot(tri, lo, preferred_element_type=F32))
            b_last = bcum[chunk - 1:chunk, :]
            q_dec = (q_ref[0, pl.ds(r0, chunk), cs] * jnp.exp(bcum)).astype(BF16)
            k_inv = (kk * jnp.exp(-bcum)).astype(BF16)
            k_end = (kk * jnp.exp(b_last - bcum)).astype(BF16)
            v = v_ref[0, pl.ds(r0, chunk), cs].astype(BF16)
            scores = jnp.where(causal, _nt_dot(q_dec, k_inv), 0.0)
            st = states[h]
            o = (jnp.dot(scores.astype(BF16), v, preferred_element_type=F32)
                 + _nt_dot(q_dec, st.astype(BF16)))
            new_states.append(
                st * jnp.exp(b_last)
                + lax.dot_general(v, k_end, (((0,), (0,)), ((), ())), preferred_element_type=F32))
            o = o * lax.rsqrt(jnp.mean(o * o, axis=-1, keepdims=True) + RMS_EPS)
            g = g_ref[0, pl.ds(r0, chunk), cs]
            o = o * ng_ref[:, cs] * (g * _sigmoid(g))
            o_ref[0, pl.ds(r0, chunk), cs] = o.astype(o_ref.dtype)
        return tuple(new_states)

    init = tuple(jnp.zeros((HG_DK, HG_DK), F32) for _ in range(heads))
    lax.fori_loop(0, lp // chunk, body, init)


def _hgrn2(proj3, lb, norm_g, width):
    nb, lp, _ = proj3.shape
    heads = 2
    bw = heads * HG_DK
    n_blk = width // bw

    def col_spec(part):
        return pl.BlockSpec((1, lp, bw), lambda b, j, part=part: (b, 0, part * n_blk + j))

    vec_spec = pl.BlockSpec((1, bw), lambda b, j: (0, j))
    return pl.pallas_call(
        functools.partial(_hgrn_kernel, heads=heads, chunk=HG_CHUNK),
        grid=(nb, n_blk),
        in_specs=[col_spec(0), col_spec(1), col_spec(2), col_spec(3), vec_spec, vec_spec],
        out_specs=pl.BlockSpec((1, lp, bw), lambda b, j: (b, 0, j)),
        out_shape=jax.ShapeDtypeStruct((nb, lp, width), BF16),
        compiler_params=pltpu.CompilerParams(
            dimension_semantics=("parallel", "parallel"), vmem_limit_bytes=VMEM_LIMIT),
        name="hgrn2_mixer",
    )(proj3, proj3, proj3, proj3, lb, norm_g)


def _block_tail_kernel(*refs, n_mix, alpha, ff_chunks):
    h_ref = refs[0]
    mix_refs = refs[1:1 + n_mix]
    w_refs = refs[1 + n_mix:1 + 2 * n_mix]
    (g1_ref, b1_ref, wup_ref, bup_ref, wdn_ref, bdn_ref, g2_ref, b2_ref, o_ref) = refs[1 + 2 * n_mix:]
    mix = jnp.dot(mix_refs[0][...], w_refs[0][...], preferred_element_type=F32)
    for m_ref, w_ref in zip(mix_refs[1:], w_refs[1:]):
        mix = mix + jnp.dot(m_ref[...], w_ref[...], preferred_element_type=F32)
    h1 = _layer_norm(alpha * h_ref[...] + mix, g1_ref[...], b1_ref[...])
    h1b = h1.astype(BF16)
    dff = wup_ref.shape[1]
    fc = dff // ff_chunks
    acc = alpha * h1 + bdn_ref[...]
    for c in range(ff_chunks):
        hid = jnp.dot(h1b, wup_ref[:, c * fc:(c + 1) * fc], preferred_element_type=F32)
        hid = jnp.maximum(hid + bup_ref[:, c * fc:(c + 1) * fc], 0.0)
        hid = (hid * hid).astype(BF16)
        acc = acc + jnp.dot(hid, wdn_ref[c * fc:(c + 1) * fc, :], preferred_element_type=F32)
    o_ref[...] = _layer_norm(acc, g2_ref[...], b2_ref[...])


def _block_tail(h, mixes, ws, g1, b1, wup, bup, wdn, bdn, g2, b2, alpha):
    rows, d = h.shape
    tm = _row_tile(rows, 512)
    row_spec = lambda n: pl.BlockSpec((tm, n), lambda i: (i, 0))
    consts = [g1, b1, wup, bup, wdn, bdn, g2, b2]
    return pl.pallas_call(
        functools.partial(_block_tail_kernel, n_mix=len(mixes), alpha=alpha, ff_chunks=4),
        grid=(rows // tm,),
        in_specs=([row_spec(d)] + [row_spec(m.shape[1]) for m in mixes]
                  + [_const_spec(w.shape) for w in ws] + [_const_spec(c.shape) for c in consts]),
        out_specs=row_spec(d),
        out_shape=jax.ShapeDtypeStruct((rows, d), F32),
        compiler_params=pltpu.CompilerParams(
            dimension_semantics=("parallel",), vmem_limit_bytes=VMEM_LIMIT),
        name="block_tail",
    )(h, *mixes, *ws, *consts)


def kernel(x, meta, w_in_ab, s5_lam_re, s5_lam_im, s5_log_dt, s5_b_re, s5_b_im, s5_c_re, s5_c_im, s5_d, s5_w_glu, s5_b_glu, w_out_ab, w_in_c, hgrn_gamma, hgrn_norm_g, w_out_c, ln_mix_g, ln_mix_b, mlp_w_up, mlp_b_up, mlp_w_down, mlp_b_down, ln_mlp_g, ln_mlp_b):
    nb, seq, d = x.shape
    depth = ln_mix_g.shape[0]
    assert depth == 2 and w_in_ab.shape[0] == 1 and w_in_c.shape[0] == 1
    alpha = (2.0 * depth) ** 0.25
    s5w = s5_w_glu.shape[1]
    sbw = (w_in_ab.shape[2] - s5w) // 3
    hgw = w_in_c.shape[2] // 4
    ltok = N_META + seq
    lp = -(-ltok // SEQ_ALIGN) * SEQ_ALIGN
    rows = nb * lp
    row2 = lambda a: a.reshape(1, -1).astype(F32)

    h = jnp.concatenate([jnp.broadcast_to(meta.astype(x.dtype)[None], (nb, N_META, d)), x,
                         jnp.zeros((nb, lp - ltok, d), x.dtype)], axis=1).reshape(rows, d)

    u, qkv = _inproj_ab(h, w_in_ab[0].astype(BF16), s5w, sbw)
    wb, wc, a_re, a_im = _s5_params(s5_lam_re[0], s5_lam_im[0], s5_log_dt[0], s5_b_re[0],
                                    s5_b_im[0], s5_c_re[0], s5_c_im[0])
    a_out = _s5(u.reshape(nb, lp, s5w), wb, a_re, a_im, wc, row2(s5_d[0]),
                s5_w_glu[0].astype(BF16), row2(s5_b_glu[0]))
    b_out = _stick_breaking(qkv.reshape(nb, lp, 3 * sbw), sbw)
    w_out = w_out_ab[0].astype(BF16)
    h = _block_tail(h, [a_out.reshape(rows, s5w), b_out.reshape(rows, sbw)],
                    [w_out[:s5w], w_out[s5w:]],
                    row2(ln_mix_g[0]), row2(ln_mix_b[0]), mlp_w_up[0].astype(BF16),
                    row2(mlp_b_up[0]), mlp_w_down[0].astype(BF16), row2(mlp_b_down[0]),
                    row2(ln_mlp_g[0]), row2(ln_mlp_b[0]), alpha)

    proj = _inproj_c(h, w_in_c[0].astype(BF16))
    p = jax.nn.softmax(hgrn_gamma.astype(F32), axis=0)
    lb = (jnp.cumsum(p, axis=0) - p[0])[1]
    c_out = _hgrn2(proj.reshape(nb, lp, 4 * hgw), row2(lb), row2(hgrn_norm_g[0]), hgw)
    h = _block_tail(h, [c_out.reshape(rows, hgw)], [w_out_c[0].astype(BF16)],
                    row2(ln_mix_g[1]), row2(ln_mix_b[1]), mlp_w_up[1].astype(BF16),
                    row2(mlp_b_up[1]), mlp_w_down[1].astype(BF16), row2(mlp_b_down[1]),
                    row2(ln_mlp_g[1]), row2(ln_mlp_b[1]), alpha)
    return h.reshape(nb, lp, d)[:, N_META:ltok, :]
```

```python
import functools
import math

import jax
import jax.numpy as jnp
from jax import lax
from jax.experimental import pallas as pl
from jax.experimental.pallas import tpu as pltpu

F32 = jnp.float32
BF16 = jnp.bfloat16

N_META = 16
S5_GROUP = 16
S5_STATE = 64
SB_HEAD_DIM = 64
HG_DK = 128
HG_CHUNK = 64
LN_EPS = 1e-5
RMS_EPS = 1e-6

LANES = 128
SEQ_ALIGN = 128
VMEM_LIMIT = 56 * 1024 * 1024


def _row_tile(rows, target):
    best = 8
    for t in range(8, min(rows, target) + 1, 8):
        if rows % t == 0:
            best = t
    return best


def _const_spec(shape):
    nd = len(shape)
    return pl.BlockSpec(shape, lambda *_: (0,) * nd, pipeline_mode=pl.Buffered(1))


def _nt_dot(a, b):
    return lax.dot_general(a, b, (((1,), (1,)), ((), ())), preferred_element_type=F32)


def _split_bf16(x):
    hi = x.astype(BF16)
    lo = (x - hi.astype(F32)).astype(BF16)
    return hi, lo


def _layer_norm(x, g, b):
    mu = jnp.mean(x, axis=-1, keepdims=True)
    xc = x - mu
    var = jnp.mean(xc * xc, axis=-1, keepdims=True)
    return xc * lax.rsqrt(var + LN_EPS) * g + b


def _sigmoid(x):
    return 1.0 / (1.0 + jnp.exp(-x))


def _inproj_ab_kernel(h_ref, w_ref, u_ref, qkv_ref, *, s5w, sbw):
    h = h_ref[...].astype(BF16)
    u_ref[...] = jnp.dot(h, w_ref[:, :s5w], preferred_element_type=F32)
    q = jnp.dot(h, w_ref[:, s5w:s5w + sbw], preferred_element_type=F32)
    qkv_ref[:, :sbw] = (q * (SB_HEAD_DIM ** -0.5)).astype(BF16)
    kv = jnp.dot(h, w_ref[:, s5w + sbw:], preferred_element_type=F32)
    qkv_ref[:, sbw:] = kv.astype(BF16)


def _inproj_ab(h, w, s5w, sbw):
    rows, d = h.shape
    tm = _row_tile(rows, 512)
    return pl.pallas_call(
        functools.partial(_inproj_ab_kernel, s5w=s5w, sbw=sbw),
        grid=(rows // tm,),
        in_specs=[pl.BlockSpec((tm, d), lambda i: (i, 0)), _const_spec(w.shape)],
        out_specs=[pl.BlockSpec((tm, s5w), lambda i: (i, 0)),
                   pl.BlockSpec((tm, 3 * sbw), lambda i: (i, 0))],
        out_shape=[jax.ShapeDtypeStruct((rows, s5w), F32),
                   jax.ShapeDtypeStruct((rows, 3 * sbw), BF16)],
        compiler_params=pltpu.CompilerParams(
            dimension_semantics=("parallel",), vmem_limit_bytes=VMEM_LIMIT),
        name="inproj_ab",
    )(h, w)


def _inproj_c_kernel(h_ref, w_ref, o_ref, *, n_split):
    h = h_ref[...].astype(BF16)
    n = w_ref.shape[1] // n_split
    for s in range(n_split):
        o_ref[:, s * n:(s + 1) * n] = jnp.dot(
            h, w_ref[:, s * n:(s + 1) * n], preferred_element_type=F32)


def _inproj_c(h, w):
    rows, d = h.shape
    n = w.shape[1]
    tm = _row_tile(rows, 512)
    return pl.pallas_call(
        functools.partial(_inproj_c_kernel, n_split=4),
        grid=(rows // tm,),
        in_specs=[pl.BlockSpec((tm, d), lambda i: (i, 0)), _const_spec(w.shape)],
        out_specs=pl.BlockSpec((tm, n), lambda i: (i, 0)),
        out_shape=jax.ShapeDtypeStruct((rows, n), F32),
        compiler_params=pltpu.CompilerParams(
            dimension_semantics=("parallel",), vmem_limit_bytes=VMEM_LIMIT),
        name="inproj_c",
    )(h, w)


S5_SCAN_UNROLL = 2


def _s5_kernel(u_ref, perm_ref, wb_ref, are_ref, aim_ref, wc_ref, d_ref, wglu_ref, bglu_ref,
               o_ref, bu_ref, sre_ref, sim_ref, ytm_ref, y_ref, *, nb, tc, slab_group):
    width = u_ref.shape[2]
    sc = are_ref.shape[1]
    nq = width // LANES
    per_q = sc // nq

    @pl.when(pl.program_id(0) == 0)
    def _():
        sre_ref[...] = jnp.zeros_like(sre_ref)
        sim_ref[...] = jnp.zeros_like(sim_ref)

    n_slabs = sc // LANES
    slabs_q = per_q // LANES
    u = u_ref[...].reshape(nb * tc, width)
    u_tm = jnp.dot(perm_ref[...], u.astype(BF16), preferred_element_type=F32).astype(BF16)
    for q in range(nq):
        r = jnp.dot(u_tm[:, q * LANES:(q + 1) * LANES], wb_ref[q], preferred_element_type=F32)
        for j in range(slabs_q):
            bu_ref[q * slabs_q + j] = r[:, j * LANES:(j + 1) * LANES]
            bu_ref[n_slabs + q * slabs_q + j] = r[:, per_q + j * LANES:per_q + (j + 1) * LANES]

    for s0 in range(0, n_slabs, slab_group):
        slabs = list(range(s0, s0 + slab_group))
        a_re = [jnp.broadcast_to(are_ref[:, s * LANES:(s + 1) * LANES], (nb, LANES)) for s in slabs]
        a_im = [jnp.broadcast_to(aim_ref[:, s * LANES:(s + 1) * LANES], (nb, LANES)) for s in slabs]
        init = []
        for s in slabs:
            init += [sre_ref[:, s * LANES:(s + 1) * LANES], sim_ref[:, s * LANES:(s + 1) * LANES]]

        def body(t, carry, slabs=slabs, a_re=a_re, a_im=a_im):
            new = []
            rows = pl.ds(pl.multiple_of(t * nb, nb), nb)
            for i, s in enumerate(slabs):
                sr, si = carry[2 * i], carry[2 * i + 1]
                nr = a_re[i] * sr - a_im[i] * si + bu_ref[s, rows, :]
                ni = a_re[i] * si + a_im[i] * sr + bu_ref[n_slabs + s, rows, :]
                bu_ref[s, rows, :] = nr
                bu_ref[n_slabs + s, rows, :] = ni
                new += [nr, ni]
            return tuple(new)

        fin = lax.fori_loop(0, tc, body, tuple(init), unroll=S5_SCAN_UNROLL)
        for i, s in enumerate(slabs):
            sre_ref[:, s * LANES:(s + 1) * LANES] = fin[2 * i]
            sim_ref[:, s * LANES:(s + 1) * LANES] = fin[2 * i + 1]

    for q in range(nq):
        s_re = jnp.concatenate(
            [bu_ref[q * slabs_q + j].astype(BF16) for j in range(slabs_q)], axis=1)
        s_im = jnp.concatenate(
            [bu_ref[n_slabs + q * slabs_q + j].astype(BF16) for j in range(slabs_q)], axis=1)
        ytm_ref[q] = (jnp.dot(s_re, wc_ref[q, :per_q, :], preferred_element_type=F32)
                      + jnp.dot(s_im, wc_ref[q, per_q:, :], preferred_element_type=F32))
    for q in range(nq):
        for b in range(nb):
            y_ref[b * tc:(b + 1) * tc, q * LANES:(q + 1) * LANES] = (
                ytm_ref[q, pl.ds(b, tc, stride=nb), :])
    y = y_ref[...] + d_ref[...] * u
    y = 0.5 * y * (1.0 + jnp.tanh(math.sqrt(2.0 / math.pi) * (y + 0.044715 * (y * y * y))))
    gate =jnp.dot(y.astype(BF16), wglu_ref[...], preferred_element_type=F32) + bglu_ref[...]
    o_ref[...] = (y * _sigmoid(gate)).reshape(nb, tc, width).astype(o_ref.dtype)


def _s5_params(lam_re, lam_im, log_dt, b_re, b_im, c_re, c_im):
    g, p = lam_re.shape
    lr, li = lam_re.astype(F32), lam_im.astype(F32)
    dt = jnp.exp(log_dt.astype(F32))[:, None]
    mag = jnp.exp(lr * dt)
    a_re = mag * jnp.cos(li * dt)
    a_im = mag * jnp.sin(li * dt)
    den = lr * lr + li * li
    c_re_ = ((a_re - 1.0) * lr + a_im * li) / den
    c_im_ = (a_im * lr - (a_re - 1.0) * li) / den
    bb_re = c_re_[:, :, None] * b_re.astype(F32) - c_im_[:, :, None] * b_im.astype(F32)
    bb_im = c_re_[:, :, None] * b_im.astype(F32) + c_im_[:, :, None] * b_re.astype(F32)
    gq = LANES // S5_GROUP
    nq = g // gq
    eye = jnp.eye(gq, dtype=F32)

    def b_slab(x):
        x = x.reshape(nq, gq, p, S5_GROUP)
        return jnp.einsum('qgph,gk->qghkp', x, eye).reshape(nq, gq * S5_GROUP, gq * p)

    def c_slab(x):
        x = x.reshape(nq, gq, S5_GROUP, p)
        return jnp.einsum('qghp,gk->qgpkh', x, eye).reshape(nq, gq * p, gq * S5_GROUP)

    wb = jnp.concatenate([b_slab(bb_re), b_slab(bb_im)], axis=2)
    wc = jnp.concatenate([c_slab(c_re.astype(F32)), c_slab(-c_im.astype(F32))], axis=1)
    return wb.astype(BF16), wc.astype(BF16), a_re.reshape(1, g * p), a_im.reshape(1, g * p)


def _s5(u3, wb, a_re, a_im, wc, d, wglu, bglu):
    nb, lp, width = u3.shape
    sc = a_re.shape[1]
    tc = 64
    assert lp % tc == 0
    rows = nb * tc
    r = jnp.arange(rows)
    perm = (jnp.arange(rows)[None, :] == ((r % nb) * tc + r // nb)[:, None]).astype(BF16)
    return pl.pallas_call(
        functools.partial(_s5_kernel, nb=nb, tc=tc, slab_group=4),
        grid=(lp // tc,),
        in_specs=[pl.BlockSpec((nb, tc, width), lambda t: (0, t, 0)), _const_spec(perm.shape),
                  _const_spec(wb.shape), _const_spec(a_re.shape), _const_spec(a_im.shape),
                  _const_spec(wc.shape), _const_spec(d.shape), _const_spec(wglu.shape),
                  _const_spec(bglu.shape)],
        out_specs=pl.BlockSpec((nb, tc, width), lambda t: (0, t, 0)),
        out_shape=jax.ShapeDtypeStruct((nb, lp, width), BF16),
        scratch_shapes=[pltpu.VMEM((2 * sc // LANES, rows, LANES), F32),
                        pltpu.VMEM((nb, sc), F32), pltpu.VMEM((nb, sc), F32),
                        pltpu.VMEM((width // LANES, rows, LANES), F32),
                        pltpu.VMEM((rows, width), F32)],
        compiler_params=pltpu.CompilerParams(
            dimension_semantics=("arbitrary",), vmem_limit_bytes=VMEM_LIMIT),
        name="s5_mixer",
    )(u3, perm, wb, a_re, a_im, wc, d, wglu, bglu)


MASKED_LOG = -1e30


def _sb_kernel(tbl_ref, q_ref, k_ref, v_ref, o_ref, q_scr, km_ref, vm_ref, carry_scr, acc_scr,
               tri2_scr, *bufs, tq, kb, n_diag, n_off):
    lp = k_ref.shape[1]
    n_blocks = km_ref.shape[0]
    n_heads = LANES // SB_HEAD_DIM
    reps = kb // LANES

    q_scr[:lp, :] = q_ref[0]
    if q_scr.shape[0] > lp:
        q_scr[lp:, :] = jnp.zeros((q_scr.shape[0] - lp, LANES), q_scr.dtype)
    carry_scr[...] = jnp.zeros_like(carry_scr)
    acc_scr[...] = jnp.zeros_like(acc_scr)
    lane = lax.broadcasted_iota(jnp.int32, (kb, LANES), 1)
    for m in range(n_blocks):
        valid = min(kb, lp - m * kb)
        k_blk = k_ref[0, m * kb:m * kb + valid, :]
        v_blk = v_ref[0, m * kb:m * kb + valid, :]
        if valid < kb:
            pad = jnp.zeros((kb - valid, LANES), k_blk.dtype)
            k_blk = jnp.concatenate([k_blk, pad], axis=0)
            v_blk = jnp.concatenate([v_blk, pad], axis=0)
        for h in range(n_heads):
            in_head = (lane >= h * SB_HEAD_DIM) & (lane < (h + 1) * SB_HEAD_DIM)
            km_ref[m, h] = jnp.where(in_head, k_blk, jnp.zeros_like(k_blk))
            vm_ref[m, h] = jnp.where(in_head, v_blk, jnp.zeros_like(v_blk))

    row = lax.broadcasted_iota(jnp.int32, (kb, kb), 0)
    col = lax.broadcasted_iota(jnp.int32, (kb, kb), 1)
    tri = jnp.where(row > col, 1.0, 0.0).astype(BF16)
    below_diag = lax.broadcasted_iota(jnp.int32, (tq, kb), 1) < lax.broadcasted_iota(
        jnp.int32, (tq, kb), 0)

    def item(j):
        return tbl_ref[3 * j], tbl_ref[3 * j + 1], tbl_ref[3 * j + 2]

    z_buf, hl_buf, base_buf, rs_buf, w_buf = (bufs[0:2], bufs[2:4], bufs[4:6], bufs[6:8],
                                              bufs[8:10])
    tri2_scr[...] = jnp.concatenate([tri, tri], axis=0)
    for ref in (z_buf[0], hl_buf[1], base_buf[1], rs_buf[1], w_buf[0]):
        ref[...] = jnp.zeros_like(ref)

    def scores(j, z_ref):
        qt, m, _ = item(j)
        q = q_scr[pl.ds(pl.multiple_of(qt * tq, tq), tq), :]
        for h in range(n_heads):
            z_ref[h] = _nt_dot(q, km_ref[m, h])

    def log_terms(z_ref, hl_ref, base_ref, rs_ref, diagonal):
        for h in range(n_heads):
            z = z_ref[h]
            nlk = jnp.maximum(z, 0.0) + jnp.log(1.0 + jnp.exp(-jnp.abs(z)))
            base = z - nlk
            if diagonal:
                nlk = jnp.where(below_diag, nlk, 0.0)
                base = jnp.where(below_diag, base, MASKED_LOG)
            hi, lo = _split_bf16(nlk)
            hl_ref[h, :, :kb] = hi
            hl_ref[h, :, kb:] = lo
            base_ref[h] = base
            rs_ref[h] = jnp.broadcast_to(jnp.sum(nlk, axis=1, keepdims=True), (tq, LANES))

    def suffix_sums(j, hl_ref):
        _, _, slot = item(j)
        return [jnp.dot(hl_ref[h], tri2_scr[...], preferred_element_type=F32)
                + jnp.concatenate([carry_scr[slot, h]] * reps, axis=1) for h in range(n_heads)]

    def weights(j, later, base_ref, rs_ref, w_ref):
        _, _, slot = item(j)
        for h in range(n_heads):
            w_ref[h] = jnp.exp(base_ref[h] - later[h]).astype(BF16)
            carry_scr[slot, h] = carry_scr[slot, h] + rs_ref[h]

    def weighted_values(j, w_ref):
        _, m, slot = item(j)
        pv = jnp.dot(w_ref[0], vm_ref[m, 0], preferred_element_type=F32)
        for h in range(1, n_heads):
            pv = pv + jnp.dot(w_ref[h], vm_ref[m, h], preferred_element_type=F32)
        acc_scr[slot] = acc_scr[slot] + pv

    def run(first, n_real, diagonal):
        n_iter = n_real + 3
        n_iter += n_iter % 2

        def step(i, p):
            j = first + i
            weighted_values(j - 2, w_buf[p])
            later = suffix_sums(j - 1, hl_buf[1 - p])
            scores(j + 1, z_buf[1 - p])
            log_terms(z_buf[p], hl_buf[p], base_buf[p], rs_buf[p], diagonal)
            weights(j - 1, later, base_buf[1 - p], rs_buf[1 - p], w_buf[1 - p])

        def body(i2, c):
            step(2 * i2, 0)
            step(2 * i2 + 1, 1)
            return c

        lax.fori_loop(0, n_iter // 2, body, 0)

    run(2, n_diag, True)
    run(n_diag + SB_DUMMIES + 2, n_off, False)
    n_q = acc_scr.shape[0] - 1
    o_ref[0] = acc_scr[:n_q].reshape(n_q * tq, LANES)[:lp].astype(o_ref.dtype)


SB_DUMMIES = 7


def _sb_items(n_q):
    dummy = [(0, 0, n_q)]
    diag = [(t, t, t) for t in range(n_q)]
    off = [(t, m, t) for t in range(n_q) for m in range(t - 1, -1, -1)]
    flat = 3 * dummy + diag + 4 * dummy + 3 * dummy + off + 4 * dummy
    return len(diag), len(off), jnp.asarray([x for e in flat for x in e], jnp.int32)


def _stick_breaking(qkv3, sbw):
    nb, lp, _ = qkv3.shape
    tq = kb = 2 * SEQ_ALIGN
    n_pairs = sbw // LANES
    n_heads = LANES // SB_HEAD_DIM
    n_q = -(-lp // tq)
    n_diag, n_off, table = _sb_items(n_q)
    masked_kv = pltpu.VMEM((n_q, n_heads, kb, LANES), BF16)
    grid_spec = pltpu.PrefetchScalarGridSpec(
        num_scalar_prefetch=1,
        grid=(nb, n_pairs),
        in_specs=[pl.BlockSpec((1, lp, LANES), lambda b, p, t: (b, 0, p)),
                  pl.BlockSpec((1, lp, LANES), lambda b, p, t: (b, 0, n_pairs + p)),
                  pl.BlockSpec((1, lp, LANES), lambda b, p, t: (b, 0, 2 * n_pairs + p))],
        out_specs=pl.BlockSpec((1, lp, LANES), lambda b, p, t: (b, 0, p)),
        scratch_shapes=[pltpu.VMEM((n_q * tq, LANES), BF16), masked_kv, masked_kv,
                        pltpu.VMEM((n_q + 1, n_heads, tq, LANES), F32),
                        pltpu.VMEM((n_q + 1, tq, LANES), F32),
                        pltpu.VMEM((2 * kb, kb), BF16)]
        + 2 * [pltpu.VMEM((n_heads, tq, kb), F32)]
        + 2 * [pltpu.VMEM((n_heads, tq, 2 * kb), BF16)]
        + 2 * [pltpu.VMEM((n_heads, tq, kb), F32)]
        + 2 * [pltpu.VMEM((n_heads, tq, LANES), F32)]
        + 2 * [pltpu.VMEM((n_heads, tq, kb), BF16)])
    return pl.pallas_call(
        functools.partial(_sb_kernel, tq=tq, kb=kb, n_diag=n_diag, n_off=n_off),
        grid_spec=grid_spec,
        out_shape=jax.ShapeDtypeStruct((nb, lp, sbw), BF16),
        compiler_params=pltpu.CompilerParams(
            dimension_semantics=("parallel", "parallel"), vmem_limit_bytes=VMEM_LIMIT),
        name="stick_breaking",
    )(table, qkv3, qkv3, qkv3)


def _hgrn_kernel(q_ref, f_ref, v_ref, g_ref, lb_ref, ng_ref, o_ref, *, chunk, group):
    lp = q_ref.shape[1]
    rows = chunk * group
    row = lax.broadcasted_iota(jnp.int32, (chunk, chunk), 0)
    col = lax.broadcasted_iota(jnp.int32, (chunk, chunk), 1)
    causal = col <= row
    tri = jnp.where(causal, 1.0, 0.0).astype(BF16)
    tri2 = jnp.concatenate([tri, tri], axis=1)
    lb = lb_ref[...]

    def body(s, st):
        sl = pl.ds(pl.multiple_of(s * rows, rows), rows)
        f = lb + (1.0 - lb) * _sigmoid(f_ref[0, sl, :])
        kk = 1.0 - f
        hi, lo = _split_bf16(jnp.log(f))
        cs = [slice(c * chunk, (c + 1) * chunk) for c in range(group)]
        bcum = jnp.concatenate(
            [jnp.dot(tri2, jnp.concatenate([hi[r], lo[r]], axis=0), preferred_element_type=F32)
             for r in cs], axis=0)
        q_dec = (q_ref[0, sl, :] * jnp.exp(bcum)).astype(BF16)
        k_inv_f = kk * jnp.exp(-bcum)
        k_inv = k_inv_f.astype(BF16)
        v = v_ref[0, sl, :].astype(BF16)
        outs = []
        for r in cs:
            e_last = jnp.exp(bcum[r.stop - 1:r.stop, :])
            k_end = (k_inv_f[r] * e_last).astype(BF16)
            scores = jnp.where(causal, _nt_dot(q_dec[r], k_inv[r]), 0.0)
            outs.append(jnp.dot(scores.astype(BF16), v[r], preferred_element_type=F32)
                        + _nt_dot(q_dec[r], st.astype(BF16)))
            st = st * e_last + lax.dot_general(v[r], k_end, (((0,), (0,)), ((), ())),
                                               preferred_element_type=F32)
        o = jnp.concatenate(outs, axis=0)
        o = o * lax.rsqrt(jnp.mean(o * o, axis=-1, keepdims=True) + RMS_EPS)
        g = g_ref[0, sl, :]
        o_ref[0, sl, :] = (o * ng_ref[...] * (g * _sigmoid(g))).astype(o_ref.dtype)
        return st

    lax.fori_loop(0, lp // rows, body, jnp.zeros((HG_DK, HG_DK), F32))


HG_MAX_GROUP = 17


def _hgrn2(proj3, lb, norm_g, width):
    nb, lp, _ = proj3.shape
    n_heads = width // HG_DK
    n_chunks = lp // HG_CHUNK
    group = max(g for g in range(1, HG_MAX_GROUP + 1) if n_chunks % g == 0)

    def col_spec(part):
        return pl.BlockSpec((1, lp, HG_DK), lambda b, j, part=part: (b, 0, part * n_heads + j))

    vec_spec = pl.BlockSpec((1, HG_DK), lambda b, j: (0, j))
    return pl.pallas_call(
        functools.partial(_hgrn_kernel, chunk=HG_CHUNK, group=group),
        grid=(nb, n_heads),
        in_specs=[col_spec(0), col_spec(1), col_spec(2), col_spec(3), vec_spec, vec_spec],
        out_specs=pl.BlockSpec((1, lp, HG_DK), lambda b, j: (b, 0, j)),
        out_shape=jax.ShapeDtypeStruct((nb, lp, width), BF16),
        compiler_params=pltpu.CompilerParams(
            dimension_semantics=("parallel", "parallel"), vmem_limit_bytes=VMEM_LIMIT),
        name="hgrn2_mixer",
    )(proj3, proj3, proj3, proj3, lb, norm_g)


def _block_tail_kernel(*refs, n_mix, alpha, ff_chunks):
    h_ref = refs[0]
    mix_refs = refs[1:1 + n_mix]
    w_refs = refs[1 + n_mix:1 + 2 * n_mix]
    (g1_ref, b1_ref, wup_ref, bup_ref, wdn_ref, bdn_ref, g2_ref, b2_ref, o_ref) = refs[1 + 2 * n_mix:]
    mix = jnp.dot(mix_refs[0][...], w_refs[0][...], preferred_element_type=F32)
    for m_ref, w_ref in zip(mix_refs[1:], w_refs[1:]):
        mix = mix + jnp.dot(m_ref[...], w_ref[...], preferred_element_type=F32)
    h1 = _layer_norm(alpha * h_ref[...] + mix, g1_ref[...], b1_ref[...])
    h1b = h1.astype(BF16)
    dff = wup_ref.shape[1]
    fc = dff // ff_chunks
    acc = alpha * h1 + bdn_ref[...]
    for c in range(ff_chunks):
        hid = jnp.dot(h1b, wup_ref[:, c * fc:(c + 1) * fc], preferred_element_type=F32)
        hid = jnp.maximum(hid + bup_ref[:, c * fc:(c + 1) * fc], 0.0)
        hid = (hid * hid).astype(BF16)
        acc = acc + jnp.dot(hid, wdn_ref[c * fc:(c + 1) * fc, :], preferred_element_type=F32)
    o_ref[...] = _layer_norm(acc, g2_ref[...], b2_ref[...])


def _block_tail(h, mixes, ws, g1, b1, wup, bup, wdn, bdn, g2, b2, alpha):
    rows, d = h.shape
    tm = _row_tile(rows, 512)
    row_spec = lambda n: pl.BlockSpec((tm, n), lambda i: (i, 0))
    consts = [g1, b1, wup, bup, wdn, bdn, g2, b2]
    return pl.pallas_call(
        functools.partial(_block_tail_kernel, n_mix=len(mixes), alpha=alpha, ff_chunks=4),
        grid=(rows // tm,),
        in_specs=([row_spec(d)] + [row_spec(m.shape[1]) for m in mixes]
                  + [_const_spec(w.shape) for w in ws] + [_const_spec(c.shape) for c in consts]),
        out_specs=row_spec(d),
        out_shape=jax.ShapeDtypeStruct((rows, d), F32),
        compiler_params=pltpu.CompilerParams(
            dimension_semantics=("parallel",), vmem_limit_bytes=VMEM_LIMIT),
        name="block_tail",
    )(h, *mixes, *ws, *consts)


def kernel(x, meta, w_in_ab, s5_lam_re, s5_lam_im, s5_log_dt, s5_b_re, s5_b_im, s5_c_re, s5_c_im, s5_d, s5_w_glu, s5_b_glu, w_out_ab, w_in_c, hgrn_gamma, hgrn_norm_g, w_out_c, ln_mix_g, ln_mix_b, mlp_w_up, mlp_b_up, mlp_w_down, mlp_b_down, ln_mlp_g, ln_mlp_b):
    nb, seq, d = x.shape
    depth = ln_mix_g.shape[0]
    assert depth == 2 and w_in_ab.shape[0] == 1 and w_in_c.shape[0] == 1
    alpha = (2.0 * depth) ** 0.25
    s5w = s5_w_glu.shape[1]
    sbw = (w_in_ab.shape[2] - s5w) // 3
    hgw = w_in_c.shape[2] // 4
    ltok = N_META + seq
    lp = -(-ltok // SEQ_ALIGN) * SEQ_ALIGN
    rows = nb * lp
    row2 = lambda a: a.reshape(1, -1).astype(F32)

    h = jnp.concatenate([jnp.broadcast_to(meta.astype(x.dtype)[None], (nb, N_META, d)), x,
                         jnp.zeros((nb, lp - ltok, d), x.dtype)], axis=1).reshape(rows, d)

    u, qkv = _inproj_ab(h, w_in_ab[0].astype(BF16), s5w, sbw)
    wb, wc, a_re, a_im = _s5_params(s5_lam_re[0], s5_lam_im[0], s5_log_dt[0], s5_b_re[0],
                                    s5_b_im[0], s5_c_re[0], s5_c_im[0])
    a_out = _s5(u.reshape(nb, lp, s5w), wb, a_re, a_im, wc, row2(s5_d[0]),
                s5_w_glu[0].astype(BF16), row2(s5_b_glu[0]))
    b_out = _stick_breaking(qkv.reshape(nb, lp, 3 * sbw), sbw)
    w_out = w_out_ab[0].astype(BF16)
    h = _block_tail(h, [a_out.reshape(rows, s5w), b_out.reshape(rows, sbw)],
                    [w_out[:s5w], w_out[s5w:]],
                    row2(ln_mix_g[0]), row2(ln_mix_b[0]), mlp_w_up[0].astype(BF16),
                    row2(mlp_b_up[0]), mlp_w_down[0].astype(BF16), row2(mlp_b_down[0]),
                    row2(ln_mlp_g[0]), row2(ln_mlp_b[0]), alpha)

    proj = _inproj_c(h, w_in_c[0].astype(BF16))
    p = jax.nn.softmax(hgrn_gamma.astype(F32), axis=0)
    lb = (jnp.cumsum(p, axis=0) - p[0])[1]
    c_out = _hgrn2(proj.reshape(nb, lp, 4 * hgw), row2(lb), row2(hgrn_norm_g[0]), hgw)
    h = _block_tail(h, [c_out.reshape(rows, hgw)], [w_out_c[0].astype(BF16)],
                    row2(ln_mix_g[1]), row2(ln_mix_b[1]), mlp_w_up[1].astype(BF16),
                    row2(mlp_b_up[1]), mlp_w_down[1].astype(BF16), row2(mlp_b_down[1]),
                    row2(ln_mlp_g[1]), row2(ln_mlp_b[1]), alpha)
    return h.reshape(nb, lp, d)[:, N_META:ltok, :]
```

```python
import functools
import math

import jax
import jax.numpy as jnp
from jax import lax
from jax.experimental import pallas as pl
from jax.experimental.pallas import tpu as pltpu

F32 = jnp.float32
BF16 = jnp.bfloat16

N_META = 16
S5_GROUP = 16
S5_STATE = 64
SB_HEAD_DIM = 64
HG_DK = 128
HG_CHUNK = 64
LN_EPS = 1e-5
RMS_EPS = 1e-6

LANES = 128
SEQ_ALIGN = 128
VMEM_LIMIT = 56 * 1024 * 1024


def _row_tile(rows, target):
    best = 8
    for t in range(8, min(rows, target) + 1, 8):
        if rows % t == 0:
            best = t
    return best


def _const_spec(shape):
    nd = len(shape)
    return pl.BlockSpec(shape, lambda *_: (0,) * nd, pipeline_mode=pl.Buffered(1))


def _nt_dot(a, b):
    return lax.dot_general(a, b, (((1,), (1,)), ((), ())), preferred_element_type=F32)


def _split_bf16(x):
    hi = x.astype(BF16)
    lo = (x - hi.astype(F32)).astype(BF16)
    return hi, lo


def _layer_norm(x, g, b):
    mu = jnp.mean(x, axis=-1, keepdims=True)
    xc = x - mu
    var = jnp.mean(xc * xc, axis=-1, keepdims=True)
    return xc * lax.rsqrt(var + LN_EPS) * g + b


def _sigmoid(x):
    return 1.0 / (1.0 + jnp.exp(-x))


def _inproj_ab_kernel(h_ref, w_ref, u_ref, qkv_ref, *, s5w, sbw):
    h = h_ref[...].astype(BF16)
    u_ref[...] = jnp.dot(h, w_ref[:, :s5w], preferred_element_type=F32)
    q = jnp.dot(h, w_ref[:, s5w:s5w + sbw], preferred_element_type=F32)
    qkv_ref[:, :sbw] = (q * (SB_HEAD_DIM ** -0.5)).astype(BF16)
    kv = jnp.dot(h, w_ref[:, s5w + sbw:], preferred_element_type=F32)
    qkv_ref[:, sbw:] = kv.astype(BF16)


def _inproj_ab(h, w, s5w, sbw):
    rows, d = h.shape
    tm = _row_tile(rows, 512)
    return pl.pallas_call(
        functools.partial(_inproj_ab_kernel, s5w=s5w, sbw=sbw),
        grid=(rows // tm,),
        in_specs=[pl.BlockSpec((tm, d), lambda i: (i, 0)), _const_spec(w.shape)],
        out_specs=[pl.BlockSpec((tm, s5w), lambda i: (i, 0)),
                   pl.BlockSpec((tm, 3 * sbw), lambda i: (i, 0))],
        out_shape=[jax.ShapeDtypeStruct((rows, s5w), F32),
                   jax.ShapeDtypeStruct((rows, 3 * sbw), BF16)],
        compiler_params=pltpu.CompilerParams(
            dimension_semantics=("parallel",), vmem_limit_bytes=VMEM_LIMIT),
        name="inproj_ab",
    )(h, w)


def _inproj_c_kernel(h_ref, w_ref, o_ref, *, n_split):
    h = h_ref[...].astype(BF16)
    n = w_ref.shape[1] // n_split
    for s in range(n_split):
        o_ref[:, s * n:(s + 1) * n] = jnp.dot(
            h, w_ref[:, s * n:(s + 1) * n], preferred_element_type=F32)


def _inproj_c(h, w):
    rows, d = h.shape
    n = w.shape[1]
    tm = _row_tile(rows, 512)
    return pl.pallas_call(
        functools.partial(_inproj_c_kernel, n_split=4),
        grid=(rows // tm,),
        in_specs=[pl.BlockSpec((tm, d), lambda i: (i, 0)), _const_spec(w.shape)],
        out_specs=pl.BlockSpec((tm, n), lambda i: (i, 0)),
        out_shape=jax.ShapeDtypeStruct((rows, n), F32),
        compiler_params=pltpu.CompilerParams(
            dimension_semantics=("parallel",), vmem_limit_bytes=VMEM_LIMIT),
        name="inproj_c",
    )(h, w)


S5_SCAN_UNROLL = 2


def _s5_kernel(u_ref, perm_ref, wb_ref, are_ref, aim_ref, wc_ref, d_ref, wglu_ref, bglu_ref,
               o_ref, bu_ref, sre_ref, sim_ref, ytm_ref, y_ref, *, nb, tc, slab_group):
    width = u_ref.shape[2]
    sc = are_ref.shape[1]
    nq = width // LANES
    per_q = sc // nq

    @pl.when(pl.program_id(0) == 0)
    def _():
        sre_ref[...] = jnp.zeros_like(sre_ref)
        sim_ref[...] = jnp.zeros_like(sim_ref)

    n_slabs = sc // LANES
    slabs_q = per_q // LANES
    u = u_ref[...].reshape(nb * tc, width)
    u_tm = jnp.dot(perm_ref[...], u.astype(BF16), preferred_element_type=F32).astype(BF16)
    for q in range(nq):
        r = jnp.dot(u_tm[:, q * LANES:(q + 1) * LANES], wb_ref[q], preferred_element_type=F32)
        for j in range(slabs_q):
            bu_ref[q * slabs_q + j] = r[:, j * LANES:(j + 1) * LANES]
            bu_ref[n_slabs + q * slabs_q + j] = r[:, per_q + j * LANES:per_q + (j + 1) * LANES]

    for s0 in range(0, n_slabs, slab_group):
        slabs = list(range(s0, s0 + slab_group))
        a_re = [jnp.broadcast_to(are_ref[:, s * LANES:(s + 1) * LANES], (nb, LANES)) for s in slabs]
        a_im = [jnp.broadcast_to(aim_ref[:, s * LANES:(s + 1) * LANES], (nb, LANES)) for s in slabs]
        init = []
        for s in slabs:
            init += [sre_ref[:, s * LANES:(s + 1) * LANES], sim_ref[:, s * LANES:(s + 1) * LANES]]

        def body(t, carry, slabs=slabs, a_re=a_re, a_im=a_im):
            new = []
            rows = pl.ds(pl.multiple_of(t * nb, nb), nb)
            for i, s in enumerate(slabs):
                sr, si = carry[2 * i], carry[2 * i + 1]
                nr = a_re[i] * sr - a_im[i] * si + bu_ref[s, rows, :]
                ni = a_re[i] * si + a_im[i] * sr + bu_ref[n_slabs + s, rows, :]
                bu_ref[s, rows, :] = nr
                bu_ref[n_slabs + s, rows, :] = ni
                new += [nr, ni]
            return tuple(new)

        fin = lax.fori_loop(0, tc, body, tuple(init), unroll=S5_SCAN_UNROLL)
        for i, s in enumerate(slabs):
            sre_ref[:, s * LANES:(s + 1) * LANES] = fin[2 * i]
            sim_ref[:, s * LANES:(s + 1) * LANES] = fin[2 * i + 1]

    for q in range(nq):
        s_re = jnp.concatenate(
            [bu_ref[q * slabs_q + j].astype(BF16) for j in range(slabs_q)], axis=1)
        s_im = jnp.concatenate(
            [bu_ref[n_slabs + q * slabs_q + j].astype(BF16) for j in range(slabs_q)], axis=1)
        ytm_ref[q] = (jnp.dot(s_re, wc_ref[q, :per_q, :], preferred_element_type=F32)
                      + jnp.dot(s_im, wc_ref[q, per_q:, :], preferred_element_type=F32))
    for q in range(nq):
        for b in range(nb):
            y_ref[b * tc:(b + 1) * tc, q * LANES:(q + 1) * LANES] = (
                ytm_ref[q, pl.ds(b, tc, stride=nb), :])
    y = y_ref[...] + d_ref[...] * u
    y = 0.5 * y * (1.0 + jnp.tanh(math.sqrt(2.0 / math.pi) * (y + 0.044715 * (y * y * y))))
    gate =jnp.dot(y.astype(BF16), wglu_ref[...], preferred_element_type=F32) + bglu_ref[...]
    o_ref[...] = (y * _sigmoid(gate)).reshape(nb, tc, width).astype(o_ref.dtype)


def _s5_params(lam_re, lam_im, log_dt, b_re, b_im, c_re, c_im):
    g, p = lam_re.shape
    lr, li = lam_re.astype(F32), lam_im.astype(F32)
    dt = jnp.exp(log_dt.astype(F32))[:, None]
    mag = jnp.exp(lr * dt)
    a_re = mag * jnp.cos(li * dt)
    a_im = mag * jnp.sin(li * dt)
    den = lr * lr + li * li
    c_re_ = ((a_re - 1.0) * lr + a_im * li) / den
    c_im_ = (a_im * lr - (a_re - 1.0) * li) / den
    bb_re = c_re_[:, :, None] * b_re.astype(F32) - c_im_[:, :, None] * b_im.astype(F32)
    bb_im = c_re_[:, :, None] * b_im.astype(F32) + c_im_[:, :, None] * b_re.astype(F32)
    gq = LANES // S5_GROUP
    nq = g // gq
    eye = jnp.eye(gq, dtype=F32)

    def b_slab(x):
        x = x.reshape(nq, gq, p, S5_GROUP)
        return jnp.einsum('qgph,gk->qghkp', x, eye).reshape(nq, gq * S5_GROUP, gq * p)

    def c_slab(x):
        x = x.reshape(nq, gq, S5_GROUP, p)
        return jnp.einsum('qghp,gk->qgpkh', x, eye).reshape(nq, gq * p, gq * S5_GROUP)

    wb = jnp.concatenate([b_slab(bb_re), b_slab(bb_im)], axis=2)
    wc = jnp.concatenate([c_slab(c_re.astype(F32)), c_slab(-c_im.astype(F32))], axis=1)
    return wb.astype(BF16), wc.astype(BF16), a_re.reshape(1, g * p), a_im.reshape(1, g * p)


def _s5(u3, wb, a_re, a_im, wc, d, wglu, bglu):
    nb, lp, width = u3.shape
    sc = a_re.shape[1]
    tc = 64
    assert lp % tc == 0
    rows = nb * tc
    r = jnp.arange(rows)
    perm = (jnp.arange(rows)[None, :] == ((r % nb) * tc + r // nb)[:, None]).astype(BF16)
    return pl.pallas_call(
        functools.partial(_s5_kernel, nb=nb, tc=tc, slab_group=4),
        grid=(lp // tc,),
        in_specs=[pl.BlockSpec((nb, tc, width), lambda t: (0, t, 0)), _const_spec(perm.shape),
                  _const_spec(wb.shape), _const_spec(a_re.shape), _const_spec(a_im.shape),
                  _const_spec(wc.shape), _const_spec(d.shape), _const_spec(wglu.shape),
                  _const_spec(bglu.shape)],
        out_specs=pl.BlockSpec((nb, tc, width), lambda t: (0, t, 0)),
        out_shape=jax.ShapeDtypeStruct((nb, lp, width), BF16),
        scratch_shapes=[pltpu.VMEM((2 * sc // LANES, rows, LANES), F32),
                        pltpu.VMEM((nb, sc), F32), pltpu.VMEM((nb, sc), F32),
                        pltpu.VMEM((width // LANES, rows, LANES), F32),
                        pltpu.VMEM((rows, width), F32)],
        compiler_params=pltpu.CompilerParams(
            dimension_semantics=("arbitrary",), vmem_limit_bytes=VMEM_LIMIT),
        name="s5_mixer",
    )(u3, perm, wb, a_re, a_im, wc, d, wglu, bglu)


MASKED_LOG = -1e30
SB_UNROLL = 4
SB_LEAD = 3
SB_TRAIL = SB_LEAD + SB_UNROLL - 1


def _sb_kernel(tbl_ref, q_ref, k_ref, v_ref, o_ref, q_scr, km_ref, vm_ref, carry_scr, acc_scr,
               tri2_scr, *bufs, tq, kb, n_diag, n_off):
    lp = k_ref.shape[1]
    n_pairs = k_ref.shape[2] // LANES
    n_q = km_ref.shape[0] // n_pairs
    n_heads = LANES // SB_HEAD_DIM
    reps = kb // LANES

    carry_scr[...] = jnp.zeros_like(carry_scr)
    acc_scr[...] = jnp.zeros_like(acc_scr)
    lane = lax.broadcasted_iota(jnp.int32, (kb, LANES), 1)
    for pair in range(n_pairs):
        cols = slice(pair * LANES, (pair + 1) * LANES)
        q0 = pair * n_q * tq
        q_scr[q0:q0 + lp, :] = q_ref[0, :, cols]
        if n_q * tq > lp:
            q_scr[q0 + lp:q0 + n_q * tq, :] = jnp.zeros((n_q * tq - lp, LANES), q_scr.dtype)
        for m in range(n_q):
            valid = min(kb, lp - m * kb)
            k_blk = k_ref[0, m * kb:m * kb + valid, cols]
            v_blk = v_ref[0, m * kb:m * kb + valid, cols]
            if valid < kb:
                pad = jnp.zeros((kb - valid, LANES), k_blk.dtype)
                k_blk = jnp.concatenate([k_blk, pad], axis=0)
                v_blk = jnp.concatenate([v_blk, pad], axis=0)
            for h in range(n_heads):
                in_head = (lane >= h * SB_HEAD_DIM) & (lane < (h + 1) * SB_HEAD_DIM)
                km_ref[pair * n_q + m, h] = jnp.where(in_head, k_blk, jnp.zeros_like(k_blk))
                vm_ref[pair * n_q + m, h] = jnp.where(in_head, v_blk, jnp.zeros_like(v_blk))

    row = lax.broadcasted_iota(jnp.int32, (kb, kb), 0)
    col = lax.broadcasted_iota(jnp.int32, (kb, kb), 1)
    tri = jnp.where(row > col, 1.0, 0.0).astype(BF16)
    below_diag = lax.broadcasted_iota(jnp.int32, (tq, kb), 1) < lax.broadcasted_iota(
        jnp.int32, (tq, kb), 0)

    def item(j):
        return tbl_ref[3 * j], tbl_ref[3 * j + 1], tbl_ref[3 * j + 2]

    z_buf, hl_buf, base_buf, rs_buf, w_buf = (bufs[0:2], bufs[2:4], bufs[4:6], bufs[6:8],
                                              bufs[8:10])
    tri2_scr[...] = jnp.concatenate([tri, tri], axis=0)
    for ref in (z_buf[0], hl_buf[1], base_buf[1], rs_buf[1], w_buf[0]):
        ref[...] = jnp.zeros_like(ref)

    def scores(j, z_ref):
        qt, m, _ = item(j)
        q = q_scr[pl.ds(pl.multiple_of(qt * tq, tq), tq), :]
        for h in range(n_heads):
            z_ref[h] = _nt_dot(q, km_ref[m, h])

    def log_terms(h, z_ref, hl_ref, base_ref, rs_ref, diagonal):
        z = z_ref[h]
        nlk = jnp.maximum(z, 0.0) + jnp.log(1.0 + jnp.exp(-jnp.abs(z)))
        base = z - nlk
        if diagonal:
            nlk = jnp.where(below_diag, nlk, 0.0)
            base = jnp.where(below_diag, base, MASKED_LOG)
        hi, lo = _split_bf16(nlk)
        hl_ref[h, :, :kb] = hi
        hl_ref[h, :, kb:] = lo
        base_ref[h] = base
        rs_ref[h] = jnp.broadcast_to(jnp.sum(nlk, axis=1, keepdims=True), (tq, LANES))

    def suffix_sum(h, j, hl_ref):
        _, _, slot = item(j)
        return (jnp.dot(hl_ref[h], tri2_scr[...], preferred_element_type=F32)
                + jnp.concatenate([carry_scr[slot, h]] * reps, axis=1))

    def weights(h, j, later, base_ref, rs_ref, w_ref):
        _, _, slot = item(j)
        w_ref[h] = jnp.exp(base_ref[h] - later).astype(BF16)
        carry_scr[slot, h] = carry_scr[slot, h] + rs_ref[h]

    def weighted_values(j, w_ref):
        _, m, slot = item(j)
        pv = jnp.dot(w_ref[0], vm_ref[m, 0], preferred_element_type=F32)
        for h in range(1, n_heads):
            pv = pv + jnp.dot(w_ref[h], vm_ref[m, h], preferred_element_type=F32)
        acc_scr[slot] = acc_scr[slot] + pv

    def run(first, n_real, diagonal):
        n_iter = -(-(n_real + SB_LEAD) // SB_UNROLL) * SB_UNROLL

        def step(i, p):
            j = first + i
            later = []
            for h in range(n_heads):
                log_terms(h, z_buf[p], hl_buf[p], base_buf[p], rs_buf[p], diagonal)
                if h == 0:
                    weighted_values(j - 2, w_buf[p])
                later.append(suffix_sum(h, j - 1, hl_buf[1 - p]))
            for h in range(n_heads):
                weights(h, j - 1, later[h], base_buf[1 - p], rs_buf[1 - p], w_buf[1 - p])
                if h == 0:
                    scores(j + 1, z_buf[1 - p])

        def body(it, c):
            for k in range(SB_UNROLL):
                step(SB_UNROLL * it + k, k % 2)
            return c

        lax.fori_loop(0, n_iter // SB_UNROLL, body, 0)

    run(SB_LEAD - 1, n_diag, True)
    run(n_diag + SB_LEAD + SB_TRAIL + SB_LEAD - 1, n_off, False)
    for pair in range(n_pairs):
        o_ref[0, :, pair * LANES:(pair + 1) * LANES] = acc_scr[
            pair * n_q:(pair + 1) * n_q].reshape(n_q * tq, LANES)[:lp].astype(o_ref.dtype)


def _sb_items(n_q, n_pairs):
    dummy = [(0, 0, n_pairs * n_q)]
    base = [p * n_q for p in range(n_pairs)]
    diag = [(o + t, o + t, o + t) for o in base for t in range(n_q)]
    off = [(o + t, o + m, o + t) for o in base for t in range(n_q) for m in range(t - 1, -1, -1)]
    flat = (SB_LEAD * dummy + diag + SB_TRAIL * dummy + SB_LEAD * dummy + off + SB_TRAIL * dummy)
    return len(diag), len(off), jnp.asarray([x for e in flat for x in e], jnp.int32)


SB_PAIRS_PER_STEP = 4


def _stick_breaking(qkv3, sbw):
    nb, lp, _ = qkv3.shape
    tq = kb = 2 * SEQ_ALIGN
    n_heads = LANES // SB_HEAD_DIM
    pps = math.gcd(SB_PAIRS_PER_STEP, sbw // LANES)
    n_steps = sbw // (pps * LANES)
    bw = pps * LANES
    n_q = -(-lp // tq)
    n_diag, n_off, table = _sb_items(n_q, pps)
    masked_kv = pltpu.VMEM((pps * n_q, n_heads, kb, LANES), BF16)
    grid_spec = pltpu.PrefetchScalarGridSpec(
        num_scalar_prefetch=1,
        grid=(nb, n_steps),
        in_specs=[pl.BlockSpec((1, lp, bw), lambda b, p, t: (b, 0, p)),
                  pl.BlockSpec((1, lp, bw), lambda b, p, t: (b, 0, n_steps + p)),
                  pl.BlockSpec((1, lp, bw), lambda b, p, t: (b, 0, 2 * n_steps + p))],
        out_specs=pl.BlockSpec((1, lp, bw), lambda b, p, t: (b, 0, p)),
        scratch_shapes=[pltpu.VMEM((pps * n_q * tq, LANES), BF16), masked_kv, masked_kv,
                        pltpu.VMEM((pps * n_q + 1, n_heads, tq, LANES), F32),
                        pltpu.VMEM((pps * n_q + 1, tq, LANES), F32),
                        pltpu.VMEM((2 * kb, kb), BF16)]
        + 2 * [pltpu.VMEM((n_heads, tq, kb), F32)]
        + 2 * [pltpu.VMEM((n_heads, tq, 2 * kb), BF16)]
        + 2 * [pltpu.VMEM((n_heads, tq, kb), F32)]
        + 2 * [pltpu.VMEM((n_heads, tq, LANES), F32)]
        + 2 * [pltpu.VMEM((n_heads, tq, kb), BF16)])
    return pl.pallas_call(
        functools.partial(_sb_kernel, tq=tq, kb=kb, n_diag=n_diag, n_off=n_off),
        grid_spec=grid_spec,
        out_shape=jax.ShapeDtypeStruct((nb, lp, sbw), BF16),
        compiler_params=pltpu.CompilerParams(
            dimension_semantics=("parallel", "parallel"), vmem_limit_bytes=VMEM_LIMIT),
        name="stick_breaking",
    )(table, qkv3, qkv3, qkv3)


def _hgrn_kernel(q_ref, f_ref, v_ref, g_ref, lb_ref, ng_ref, o_ref, *, chunk, group):
    lp = q_ref.shape[1]
    rows = chunk * group
    row = lax.broadcasted_iota(jnp.int32, (chunk, chunk), 0)
    col = lax.broadcasted_iota(jnp.int32, (chunk, chunk), 1)
    causal = col <= row
    tri = jnp.where(causal, 1.0, 0.0).astype(BF16)
    tri2 = jnp.concatenate([tri, tri], axis=1)
    lb = lb_ref[...]

    def body(s, st):
        sl = pl.ds(pl.multiple_of(s * rows, rows), rows)
        f = lb + (1.0 - lb) * _sigmoid(f_ref[0, sl, :])
        kk = 1.0 - f
        hi, lo = _split_bf16(jnp.log(f))
        cs = [slice(c * chunk, (c + 1) * chunk) for c in range(group)]
        bcum = jnp.concatenate(
            [jnp.dot(tri2, jnp.concatenate([hi[r], lo[r]], axis=0), preferred_element_type=F32)
             for r in cs], axis=0)
        q_dec = (q_ref[0, sl, :] * jnp.exp(bcum)).astype(BF16)
        k_inv_f = kk * jnp.exp(-bcum)
        k_inv = k_inv_f.astype(BF16)
        v = v_ref[0, sl, :].astype(BF16)
        outs = []
        for r in cs:
            e_last = jnp.exp(bcum[r.stop - 1:r.stop, :])
            k_end = (k_inv_f[r] * e_last).astype(BF16)
            scores = jnp.where(causal, _nt_dot(q_dec[r], k_inv[r]), 0.0)
            outs.append(jnp.dot(scores.astype(BF16), v[r], preferred_element_type=F32)
                        + _nt_dot(q_dec[r], st.astype(BF16)))
            st = st * e_last + lax.dot_general(v[r], k_end, (((0,), (0,)), ((), ())),
                                               preferred_element_type=F32)
        o = jnp.concatenate(outs, axis=0)
        o = o * lax.rsqrt(jnp.mean(o * o, axis=-1, keepdims=True) + RMS_EPS)
        g = g_ref[0, sl, :]
        o_ref[0, sl, :] = (o * ng_ref[...] * (g * _sigmoid(g))).astype(o_ref.dtype)
        return st

    lax.fori_loop(0, lp // rows, body, jnp.zeros((HG_DK, HG_DK), F32))


HG_MAX_GROUP = 17


def _hgrn2(proj3, lb, norm_g, width):
    nb, lp, _ = proj3.shape
    n_heads = width // HG_DK
    n_chunks = lp // HG_CHUNK
    group = max(g for g in range(1, HG_MAX_GROUP + 1) if n_chunks % g == 0)

    def col_spec(part):
        return pl.BlockSpec((1, lp, HG_DK), lambda b, j, part=part: (b, 0, part * n_heads + j))

    vec_spec = pl.BlockSpec((1, HG_DK), lambda b, j: (0, j))
    return pl.pallas_call(
        functools.partial(_hgrn_kernel, chunk=HG_CHUNK, group=group),
        grid=(nb, n_heads),
        in_specs=[col_spec(0), col_spec(1), col_spec(2), col_spec(3), vec_spec, vec_spec],
        out_specs=pl.BlockSpec((1, lp, HG_DK), lambda b, j: (b, 0, j)),
        out_shape=jax.ShapeDtypeStruct((nb, lp, width), BF16),
        compiler_params=pltpu.CompilerParams(
            dimension_semantics=("parallel", "parallel"), vmem_limit_bytes=VMEM_LIMIT),
        name="hgrn2_mixer",
    )(proj3, proj3, proj3, proj3, lb, norm_g)


def _block_tail_kernel(*refs, n_mix, alpha, ff_chunks):
    h_ref = refs[0]
    mix_refs = refs[1:1 + n_mix]
    w_refs = refs[1 + n_mix:1 + 2 * n_mix]
    (g1_ref, b1_ref, wup_ref, bup_ref, wdn_ref, bdn_ref, g2_ref, b2_ref, o_ref) = refs[1 + 2 * n_mix:]
    mix = jnp.dot(mix_refs[0][...], w_refs[0][...], preferred_element_type=F32)
    for m_ref, w_ref in zip(mix_refs[1:], w_refs[1:]):
        mix = mix + jnp.dot(m_ref[...], w_ref[...], preferred_element_type=F32)
    h1 = _layer_norm(alpha * h_ref[...] + mix, g1_ref[...], b1_ref[...])
    h1b = h1.astype(BF16)
    dff = wup_ref.shape[1]
    fc = dff // ff_chunks
    acc = alpha * h1 + bdn_ref[...]
    for c in range(ff_chunks):
        hid = jnp.dot(h1b, wup_ref[:, c * fc:(c + 1) * fc], preferred_element_type=F32)
        hid = jnp.maximum(hid + bup_ref[:, c * fc:(c + 1) * fc], 0.0)
        hid = (hid * hid).astype(BF16)
        acc = acc + jnp.dot(hid, wdn_ref[c * fc:(c + 1) * fc, :], preferred_element_type=F32)
    o_ref[...] = _layer_norm(acc, g2_ref[...], b2_ref[...])


def _block_tail(h, mixes, ws, g1, b1, wup, bup, wdn, bdn, g2, b2, alpha):
    rows, d = h.shape
    tm = _row_tile(rows, 512)
    row_spec = lambda n: pl.BlockSpec((tm, n), lambda i: (i, 0))
    consts = [g1, b1, wup, bup, wdn, bdn, g2, b2]
    return pl.pallas_call(
        functools.partial(_block_tail_kernel, n_mix=len(mixes), alpha=alpha, ff_chunks=4),
        grid=(rows // tm,),
        in_specs=([row_spec(d)] + [row_spec(m.shape[1]) for m in mixes]
                  + [_const_spec(w.shape) for w in ws] + [_const_spec(c.shape) for c in consts]),
        out_specs=row_spec(d),
        out_shape=jax.ShapeDtypeStruct((rows, d), F32),
        compiler_params=pltpu.CompilerParams(
            dimension_semantics=("parallel",), vmem_limit_bytes=VMEM_LIMIT),
        name="block_tail",
    )(h, *mixes, *ws, *consts)


def kernel(x, meta, w_in_ab, s5_lam_re, s5_lam_im, s5_log_dt, s5_b_re, s5_b_im, s5_c_re, s5_c_im, s5_d, s5_w_glu, s5_b_glu, w_out_ab, w_in_c, hgrn_gamma, hgrn_norm_g, w_out_c, ln_mix_g, ln_mix_b, mlp_w_up, mlp_b_up, mlp_w_down, mlp_b_down, ln_mlp_g, ln_mlp_b):
    nb, seq, d = x.shape
    depth = ln_mix_g.shape[0]
    assert depth == 2 and w_in_ab.shape[0] == 1 and w_in_c.shape[0] == 1
    alpha = (2.0 * depth) ** 0.25
    s5w = s5_w_glu.shape[1]
    sbw = (w_in_ab.shape[2] - s5w) // 3
    hgw = w_in_c.shape[2] // 4
    ltok = N_META + seq
    lp = -(-ltok // SEQ_ALIGN) * SEQ_ALIGN
    rows = nb * lp
    row2 = lambda a: a.reshape(1, -1).astype(F32)

    h = jnp.concatenate([jnp.broadcast_to(meta.astype(x.dtype)[None], (nb, N_META, d)), x,
                         jnp.zeros((nb, lp - ltok, d), x.dtype)], axis=1).reshape(rows, d)

    u, qkv = _inproj_ab(h, w_in_ab[0].astype(BF16), s5w, sbw)
    wb, wc, a_re, a_im = _s5_params(s5_lam_re[0], s5_lam_im[0], s5_log_dt[0], s5_b_re[0],
                                    s5_b_im[0], s5_c_re[0], s5_c_im[0])
    a_out = _s5(u.reshape(nb, lp, s5w), wb, a_re, a_im, wc, row2(s5_d[0]),
                s5_w_glu[0].astype(BF16), row2(s5_b_glu[0]))
    b_out = _stick_breaking(qkv.reshape(nb, lp, 3 * sbw), sbw)
    w_out = w_out_ab[0].astype(BF16)
    h = _block_tail(h, [a_out.reshape(rows, s5w), b_out.reshape(rows, sbw)],
                    [w_out[:s5w], w_out[s5w:]],
                    row2(ln_mix_g[0]), row2(ln_mix_b[0]), mlp_w_up[0].astype(BF16),
                    row2(mlp_b_up[0]), mlp_w_down[0].astype(BF16), row2(mlp_b_down[0]),
                    row2(ln_mlp_g[0]), row2(ln_mlp_b[0]), alpha)

    proj = _inproj_c(h, w_in_c[0].astype(BF16))
    p = jax.nn.softmax(hgrn_gamma.astype(F32), axis=0)
    lb = (jnp.cumsum(p, axis=0) - p[0])[1]
    c_out = _hgrn2(proj.reshape(nb, lp, 4 * hgw), row2(lb), row2(hgrn_norm_g[0]), hgw)
    h = _block_tail(h, [c_out.reshape(rows, hgw)], [w_out_c[0].astype(BF16)],
                    row2(ln_mix_g[1]), row2(ln_mix_b[1]), mlp_w_up[1].astype(BF16),
                    row2(mlp_b_up[1]), mlp_w_down[1].astype(BF16), row2(mlp_b_down[1]),
                    row2(ln_mlp_g[1]), row2(ln_mlp_b[1]), alpha)
    return h.reshape(nb, lp, d)[:, N_META:ltok, :]
```

```python
import functools
import math

import jax
import jax.numpy as jnp
from jax import lax
from jax.experimental import pallas as pl
from jax.experimental.pallas import tpu as pltpu

F32 = jnp.float32
BF16 = jnp.bfloat16

N_META = 16
S5_GROUP = 16
S5_STATE = 64
SB_HEAD_DIM = 64
HG_DK = 128
HG_CHUNK = 64
LN_EPS = 1e-5
RMS_EPS = 1e-6

LANES = 128
SEQ_ALIGN = 128
VMEM_LIMIT = 56 * 1024 * 1024


def _row_tile(rows, target):
    best = 8
    for t in range(8, min(rows, target) + 1, 8):
        if rows % t == 0:
            best = t
    return best


def _const_spec(shape):
    nd = len(shape)
    return pl.BlockSpec(shape, lambda *_: (0,) * nd, pipeline_mode=pl.Buffered(1))


def _nt_dot(a, b):
    return lax.dot_general(a, b, (((1,), (1,)), ((), ())), preferred_element_type=F32)


def _split_bf16(x):
    hi = x.astype(BF16)
    lo = (x - hi.astype(F32)).astype(BF16)
    return hi, lo


def _layer_norm(x, g, b):
    mu = jnp.mean(x, axis=-1, keepdims=True)
    xc = x - mu
    var = jnp.mean(xc * xc, axis=-1, keepdims=True)
    return xc * lax.rsqrt(var + LN_EPS) * g + b


def _sigmoid(x):
    return 1.0 / (1.0 + jnp.exp(-x))


def _inproj_ab_kernel(h_ref, w_ref, u_ref, qkv_ref, *, s5w, sbw):
    h = h_ref[...].astype(BF16)
    u_ref[...] = jnp.dot(h, w_ref[:, :s5w], preferred_element_type=F32)
    q = jnp.dot(h, w_ref[:, s5w:s5w + sbw], preferred_element_type=F32)
    qkv_ref[:, :sbw] = (q * (SB_HEAD_DIM ** -0.5)).astype(BF16)
    kv = jnp.dot(h, w_ref[:, s5w + sbw:], preferred_element_type=F32)
    qkv_ref[:, sbw:] = kv.astype(BF16)


def _inproj_ab(h, w, s5w, sbw):
    rows, d = h.shape
    tm = _row_tile(rows, 512)
    return pl.pallas_call(
        functools.partial(_inproj_ab_kernel, s5w=s5w, sbw=sbw),
        grid=(rows // tm,),
        in_specs=[pl.BlockSpec((tm, d), lambda i: (i, 0)), _const_spec(w.shape)],
        out_specs=[pl.BlockSpec((tm, s5w), lambda i: (i, 0)),
                   pl.BlockSpec((tm, 3 * sbw), lambda i: (i, 0))],
        out_shape=[jax.ShapeDtypeStruct((rows, s5w), F32),
                   jax.ShapeDtypeStruct((rows, 3 * sbw), BF16)],
        compiler_params=pltpu.CompilerParams(
            dimension_semantics=("parallel",), vmem_limit_bytes=VMEM_LIMIT),
        name="inproj_ab",
    )(h, w)


def _inproj_c_kernel(h_ref, w_ref, o_ref, *, n_split):
    h = h_ref[...].astype(BF16)
    n = w_ref.shape[1] // n_split
    for s in range(n_split):
        o_ref[:, s * n:(s + 1) * n] = jnp.dot(
            h, w_ref[:, s * n:(s + 1) * n], preferred_element_type=F32)


def _inproj_c(h, w):
    rows, d = h.shape
    n = w.shape[1]
    tm = _row_tile(rows, 512)
    return pl.pallas_call(
        functools.partial(_inproj_c_kernel, n_split=4),
        grid=(rows // tm,),
        in_specs=[pl.BlockSpec((tm, d), lambda i: (i, 0)), _const_spec(w.shape)],
        out_specs=pl.BlockSpec((tm, n), lambda i: (i, 0)),
        out_shape=jax.ShapeDtypeStruct((rows, n), F32),
        compiler_params=pltpu.CompilerParams(
            dimension_semantics=("parallel",), vmem_limit_bytes=VMEM_LIMIT),
        name="inproj_c",
    )(h, w)


S5_SCAN_UNROLL = 2


def _s5_kernel(u_ref, perm_ref, wb_ref, are_ref, aim_ref, wc_ref, d_ref, wglu_ref, bglu_ref,
               o_ref, bu_ref, sre_ref, sim_ref, ytm_ref, y_ref, *, nb, tc, slab_group):
    width = u_ref.shape[2]
    sc = are_ref.shape[1]
    nq = width // LANES
    per_q = sc // nq

    @pl.when(pl.program_id(0) == 0)
    def _():
        sre_ref[...] = jnp.zeros_like(sre_ref)
        sim_ref[...] = jnp.zeros_like(sim_ref)

    n_slabs = sc // LANES
    slabs_q = per_q // LANES
    u = u_ref[...].reshape(nb * tc, width)
    u_tm = jnp.dot(perm_ref[...], u.astype(BF16), preferred_element_type=F32).astype(BF16)
    for q in range(nq):
        r = jnp.dot(u_tm[:, q * LANES:(q + 1) * LANES], wb_ref[q], preferred_element_type=F32)
        for j in range(slabs_q):
            bu_ref[q * slabs_q + j] = r[:, j * LANES:(j + 1) * LANES]
            bu_ref[n_slabs + q * slabs_q + j] = r[:, per_q + j * LANES:per_q + (j + 1) * LANES]

    for s0 in range(0, n_slabs, slab_group):
        slabs = list(range(s0, s0 + slab_group))
        a_re = [jnp.broadcast_to(are_ref[:, s * LANES:(s + 1) * LANES], (nb, LANES)) for s in slabs]
        a_im = [jnp.broadcast_to(aim_ref[:, s * LANES:(s + 1) * LANES], (nb, LANES)) for s in slabs]
        init = []
        for s in slabs:
            init += [sre_ref[:, s * LANES:(s + 1) * LANES], sim_ref[:, s * LANES:(s + 1) * LANES]]

        def body(t, carry, slabs=slabs, a_re=a_re, a_im=a_im):
            new = []
            rows = pl.ds(pl.multiple_of(t * nb, nb), nb)
            for i, s in enumerate(slabs):
                sr, si = carry[2 * i], carry[2 * i + 1]
                nr = a_re[i] * sr - a_im[i] * si + bu_ref[s, rows, :]
                ni = a_re[i] * si + a_im[i] * sr + bu_ref[n_slabs + s, rows, :]
                bu_ref[s, rows, :] = nr
                bu_ref[n_slabs + s, rows, :] = ni
                new += [nr, ni]
            return tuple(new)

        fin = lax.fori_loop(0, tc, body, tuple(init), unroll=S5_SCAN_UNROLL)
        for i, s in enumerate(slabs):
            sre_ref[:, s * LANES:(s + 1) * LANES] = fin[2 * i]
            sim_ref[:, s * LANES:(s + 1) * LANES] = fin[2 * i + 1]

    for q in range(nq):
        s_re = jnp.concatenate(
            [bu_ref[q * slabs_q + j].astype(BF16) for j in range(slabs_q)], axis=1)
        s_im = jnp.concatenate(
            [bu_ref[n_slabs + q * slabs_q + j].astype(BF16) for j in range(slabs_q)], axis=1)
        ytm_ref[q] = (jnp.dot(s_re, wc_ref[q, :per_q, :], preferred_element_type=F32)
                      + jnp.dot(s_im, wc_ref[q, per_q:, :], preferred_element_type=F32))
    for q in range(nq):
        for b in range(nb):
            y_ref[b * tc:(b + 1) * tc, q * LANES:(q + 1) * LANES] = (
                ytm_ref[q, pl.ds(b, tc, stride=nb), :])
    y = y_ref[...] + d_ref[...] * u
    y = 0.5 * y * (1.0 + jnp.tanh(math.sqrt(2.0 / math.pi) * (y + 0.044715 * (y * y * y))))
    gate =jnp.dot(y.astype(BF16), wglu_ref[...], preferred_element_type=F32) + bglu_ref[...]
    o_ref[...] = (y * _sigmoid(gate)).reshape(nb, tc, width).astype(o_ref.dtype)


def _s5_params(lam_re, lam_im, log_dt, b_re, b_im, c_re, c_im):
    g, p = lam_re.shape
    lr, li = lam_re.astype(F32), lam_im.astype(F32)
    dt = jnp.exp(log_dt.astype(F32))[:, None]
    mag = jnp.exp(lr * dt)
    a_re = mag * jnp.cos(li * dt)
    a_im = mag * jnp.sin(li * dt)
    den = lr * lr + li * li
    c_re_ = ((a_re - 1.0) * lr + a_im * li) / den
    c_im_ = (a_im * lr - (a_re - 1.0) * li) / den
    bb_re = c_re_[:, :, None] * b_re.astype(F32) - c_im_[:, :, None] * b_im.astype(F32)
    bb_im = c_re_[:, :, None] * b_im.astype(F32) + c_im_[:, :, None] * b_re.astype(F32)
    gq = LANES // S5_GROUP
    nq = g // gq
    eye = jnp.eye(gq, dtype=F32)

    def b_slab(x):
        x = x.reshape(nq, gq, p, S5_GROUP)
        return jnp.einsum('qgph,gk->qghkp', x, eye).reshape(nq, gq * S5_GROUP, gq * p)

    def c_slab(x):
        x = x.reshape(nq, gq, S5_GROUP, p)
        return jnp.einsum('qghp,gk->qgpkh', x, eye).reshape(nq, gq * p, gq * S5_GROUP)

    wb = jnp.concatenate([b_slab(bb_re), b_slab(bb_im)], axis=2)
    wc = jnp.concatenate([c_slab(c_re.astype(F32)), c_slab(-c_im.astype(F32))], axis=1)
    return wb.astype(BF16), wc.astype(BF16), a_re.reshape(1, g * p), a_im.reshape(1, g * p)


def _s5(u3, wb, a_re, a_im, wc, d, wglu, bglu):
    nb, lp, width = u3.shape
    sc = a_re.shape[1]
    tc = 64
    assert lp % tc == 0
    rows = nb * tc
    r = jnp.arange(rows)
    perm = (jnp.arange(rows)[None, :] == ((r % nb) * tc + r // nb)[:, None]).astype(BF16)
    return pl.pallas_call(
        functools.partial(_s5_kernel, nb=nb, tc=tc, slab_group=4),
        grid=(lp // tc,),
        in_specs=[pl.BlockSpec((nb, tc, width), lambda t: (0, t, 0)), _const_spec(perm.shape),
                  _const_spec(wb.shape), _const_spec(a_re.shape), _const_spec(a_im.shape),
                  _const_spec(wc.shape), _const_spec(d.shape), _const_spec(wglu.shape),
                  _const_spec(bglu.shape)],
        out_specs=pl.BlockSpec((nb, tc, width), lambda t: (0, t, 0)),
        out_shape=jax.ShapeDtypeStruct((nb, lp, width), BF16),
        scratch_shapes=[pltpu.VMEM((2 * sc // LANES, rows, LANES), F32),
                        pltpu.VMEM((nb, sc), F32), pltpu.VMEM((nb, sc), F32),
                        pltpu.VMEM((width // LANES, rows, LANES), F32),
                        pltpu.VMEM((rows, width), F32)],
        compiler_params=pltpu.CompilerParams(
            dimension_semantics=("arbitrary",), vmem_limit_bytes=VMEM_LIMIT),
        name="s5_mixer",
    )(u3, perm, wb, a_re, a_im, wc, d, wglu, bglu)


MASKED_LOG = -1e30
SB_UNROLL = 4
SB_LEAD = 3
SB_TRAIL = SB_LEAD + SB_UNROLL - 1


def _sb_kernel(tbl_ref, q_ref, k_ref, v_ref, o_ref, q_scr, km_ref, vm_ref, carry_scr, acc_scr,
               tri_scr, *bufs, tq, kb, n_diag, n_off):
    lp = k_ref.shape[1]
    n_pairs = k_ref.shape[2] // LANES
    n_q = km_ref.shape[0] // n_pairs
    n_heads = LANES // SB_HEAD_DIM
    reps = kb // LANES

    carry_scr[...] = jnp.zeros_like(carry_scr)
    acc_scr[...] = jnp.zeros_like(acc_scr)
    lane = lax.broadcasted_iota(jnp.int32, (kb, LANES), 1)
    for pair in range(n_pairs):
        cols = slice(pair * LANES, (pair + 1) * LANES)
        q0 = pair * n_q * tq
        q_scr[q0:q0 + lp, :] = q_ref[0, :, cols]
        if n_q * tq > lp:
            q_scr[q0 + lp:q0 + n_q * tq, :] = jnp.zeros((n_q * tq - lp, LANES), q_scr.dtype)
        for m in range(n_q):
            valid = min(kb, lp - m * kb)
            k_blk = k_ref[0, m * kb:m * kb + valid, cols]
            v_blk = v_ref[0, m * kb:m * kb + valid, cols]
            if valid < kb:
                pad = jnp.zeros((kb - valid, LANES), k_blk.dtype)
                k_blk = jnp.concatenate([k_blk, pad], axis=0)
                v_blk = jnp.concatenate([v_blk, pad], axis=0)
            for h in range(n_heads):
                in_head = (lane >= h * SB_HEAD_DIM) & (lane < (h + 1) * SB_HEAD_DIM)
                km_ref[pair * n_q + m, h] = jnp.where(in_head, k_blk, jnp.zeros_like(k_blk))
                vm_ref[pair * n_q + m, h] = jnp.where(in_head, v_blk, jnp.zeros_like(v_blk))

    row = lax.broadcasted_iota(jnp.int32, (kb, kb), 0)
    col = lax.broadcasted_iota(jnp.int32, (kb, kb), 1)
    tri = jnp.where(row > col, 1.0, 0.0).astype(BF16)
    below_diag = lax.broadcasted_iota(jnp.int32, (tq, kb), 1) < lax.broadcasted_iota(
        jnp.int32, (tq, kb), 0)

    def item(j):
        return tbl_ref[3 * j], tbl_ref[3 * j + 1], tbl_ref[3 * j + 2]

    z_buf, hl_buf, base_buf, rs_buf, w_buf = (bufs[0:2], bufs[2:4], bufs[4:6], bufs[6:8],
                                              bufs[8:10])
    tri_scr[...] = tri
    for ref in (z_buf[0], hl_buf[1], base_buf[1], rs_buf[1], w_buf[0]):
        ref[...] = jnp.zeros_like(ref)

    def scores(j, z_ref):
        qt, m, _ = item(j)
        q = q_scr[pl.ds(pl.multiple_of(qt * tq, tq), tq), :]
        for h in range(n_heads):
            z_ref[h] = _nt_dot(q, km_ref[m, h])

    def log_terms(h, z_ref, hl_ref, base_ref, rs_ref, diagonal):
        z = z_ref[h]
        nlk = jnp.maximum(z, 0.0) + jnp.log(1.0 + jnp.exp(-jnp.abs(z)))
        base = z - nlk
        if diagonal:
            nlk = jnp.where(below_diag, nlk, 0.0)
            base = jnp.where(below_diag, base, MASKED_LOG)
        hl_ref[h] = nlk.astype(BF16)
        base_ref[h] = base
        rs_ref[h] = jnp.broadcast_to(jnp.sum(nlk, axis=1, keepdims=True), (tq, LANES))

    def suffix_sum(h, j, hl_ref):
        _, _, slot = item(j)
        return (jnp.dot(hl_ref[h], tri_scr[...], preferred_element_type=F32)
                + jnp.concatenate([carry_scr[slot, h]] * reps, axis=1))

    def weights(h, j, later, base_ref, rs_ref, w_ref):
        _, _, slot = item(j)
        w_ref[h] = jnp.exp(base_ref[h] - later).astype(BF16)
        carry_scr[slot, h] = carry_scr[slot, h] + rs_ref[h]

    def weighted_values(j, w_ref):
        _, m, slot = item(j)
        pv = jnp.dot(w_ref[0], vm_ref[m, 0], preferred_element_type=F32)
        for h in range(1, n_heads):
            pv = pv + jnp.dot(w_ref[h], vm_ref[m, h], preferred_element_type=F32)
        acc_scr[slot] = acc_scr[slot] + pv

    def run(first, n_real, diagonal):
        n_iter = -(-(n_real + SB_LEAD) // SB_UNROLL) * SB_UNROLL

        def step(i, p):
            j = first + i
            later = []
            for h in range(n_heads):
                log_terms(h, z_buf[p], hl_buf[p], base_buf[p], rs_buf[p], diagonal)
                if h == 0:
                    weighted_values(j - 2, w_buf[p])
                later.append(suffix_sum(h, j - 1, hl_buf[1 - p]))
            for h in range(n_heads):
                weights(h, j - 1, later[h], base_buf[1 - p], rs_buf[1 - p], w_buf[1 - p])
                if h == 0:
                    scores(j + 1, z_buf[1 - p])

        def body(it, c):
            for k in range(SB_UNROLL):
                step(SB_UNROLL * it + k, k % 2)
            return c

        lax.fori_loop(0, n_iter // SB_UNROLL, body, 0)

    run(SB_LEAD - 1, n_diag, True)
    run(n_diag + SB_LEAD + SB_TRAIL + SB_LEAD - 1, n_off, False)
    for pair in range(n_pairs):
        o_ref[0, :, pair * LANES:(pair + 1) * LANES] = acc_scr[
            pair * n_q:(pair + 1) * n_q].reshape(n_q * tq, LANES)[:lp].astype(o_ref.dtype)


def _sb_items(n_q, n_pairs):
    dummy = [(0, 0, n_pairs * n_q)]
    base = [p * n_q for p in range(n_pairs)]
    diag = [(o + t, o + t, o + t) for o in base for t in range(n_q)]
    off = [(o + t, o + m, o + t) for o in base for t in range(n_q) for m in range(t - 1, -1, -1)]
    flat = (SB_LEAD * dummy + diag + SB_TRAIL * dummy + SB_LEAD * dummy + off + SB_TRAIL * dummy)
    return len(diag), len(off), jnp.asarray([x for e in flat for x in e], jnp.int32)


SB_PAIRS_PER_STEP = 4


def _stick_breaking(qkv3, sbw):
    nb, lp, _ = qkv3.shape
    tq = kb = 2 * SEQ_ALIGN
    n_heads = LANES // SB_HEAD_DIM
    pps = math.gcd(SB_PAIRS_PER_STEP, sbw // LANES)
    n_steps = sbw // (pps * LANES)
    bw = pps * LANES
    n_q = -(-lp // tq)
    n_diag, n_off, table = _sb_items(n_q, pps)
    masked_kv = pltpu.VMEM((pps * n_q, n_heads, kb, LANES), BF16)
    grid_spec = pltpu.PrefetchScalarGridSpec(
        num_scalar_prefetch=1,
        grid=(nb, n_steps),
        in_specs=[pl.BlockSpec((1, lp, bw), lambda b, p, t: (b, 0, p)),
                  pl.BlockSpec((1, lp, bw), lambda b, p, t: (b, 0, n_steps + p)),
                  pl.BlockSpec((1, lp, bw), lambda b, p, t: (b, 0, 2 * n_steps + p))],
        out_specs=pl.BlockSpec((1, lp, bw), lambda b, p, t: (b, 0, p)),
        scratch_shapes=[pltpu.VMEM((pps * n_q * tq, LANES), BF16), masked_kv, masked_kv,
                        pltpu.VMEM((pps * n_q + 1, n_heads, tq, LANES), F32),
                        pltpu.VMEM((pps * n_q + 1, tq, LANES), F32),
                        pltpu.VMEM((kb, kb), BF16)]
        + 2 * [pltpu.VMEM((n_heads, tq, kb), F32)]
        + 2 * [pltpu.VMEM((n_heads, tq, kb), BF16)]
        + 2 * [pltpu.VMEM((n_heads, tq, kb), F32)]
        + 2 * [pltpu.VMEM((n_heads, tq, LANES), F32)]
        + 2 * [pltpu.VMEM((n_heads, tq, kb), BF16)])
    return pl.pallas_call(
        functools.partial(_sb_kernel, tq=tq, kb=kb, n_diag=n_diag, n_off=n_off),
        grid_spec=grid_spec,
        out_shape=jax.ShapeDtypeStruct((nb, lp, sbw), BF16),
        compiler_params=pltpu.CompilerParams(
            dimension_semantics=("parallel", "parallel"), vmem_limit_bytes=VMEM_LIMIT),
        name="stick_breaking",
    )(table, qkv3, qkv3, qkv3)


def _hgrn_kernel(q_ref, f_ref, v_ref, g_ref, lb_ref, ng_ref, o_ref, *, chunk, group):
    lp = q_ref.shape[1]
    rows = chunk * group
    row = lax.broadcasted_iota(jnp.int32, (chunk, chunk), 0)
    col = lax.broadcasted_iota(jnp.int32, (chunk, chunk), 1)
    causal = col <= row
    tri = jnp.where(causal, 1.0, 0.0).astype(BF16)
    tri2 = jnp.concatenate([tri, tri], axis=1)
    lb = lb_ref[...]

    def body(s, st):
        sl = pl.ds(pl.multiple_of(s * rows, rows), rows)
        f = lb + (1.0 - lb) * _sigmoid(f_ref[0, sl, :])
        kk = 1.0 - f
        hi, lo = _split_bf16(jnp.log(f))
        cs = [slice(c * chunk, (c + 1) * chunk) for c in range(group)]
        bcum = jnp.concatenate(
            [jnp.dot(tri2, jnp.concatenate([hi[r], lo[r]], axis=0), preferred_element_type=F32)
             for r in cs], axis=0)
        q_dec = (q_ref[0, sl, :] * jnp.exp(bcum)).astype(BF16)
        k_inv_f = kk * jnp.exp(-bcum)
        k_inv = k_inv_f.astype(BF16)
        v = v_ref[0, sl, :].astype(BF16)
        outs = []
        for r in cs:
            e_last = jnp.exp(bcum[r.stop - 1:r.stop, :])
            k_end = (k_inv_f[r] * e_last).astype(BF16)
            scores = jnp.where(causal, _nt_dot(q_dec[r], k_inv[r]), 0.0)
            outs.append(jnp.dot(scores.astype(BF16), v[r], preferred_element_type=F32)
                        + _nt_dot(q_dec[r], st.astype(BF16)))
            st = st * e_last + lax.dot_general(v[r], k_end, (((0,), (0,)), ((), ())),
                                               preferred_element_type=F32)
        o = jnp.concatenate(outs, axis=0)
        o = o * lax.rsqrt(jnp.mean(o * o, axis=-1, keepdims=True) + RMS_EPS)
        g = g_ref[0, sl, :]
        o_ref[0, sl, :] = (o * ng_ref[...] * (g * _sigmoid(g))).astype(o_ref.dtype)
        return st

    lax.fori_loop(0, lp // rows, body, jnp.zeros((HG_DK, HG_DK), F32))


HG_MAX_GROUP = 17


def _hgrn2(proj3, lb, norm_g, width):
    nb, lp, _ = proj3.shape
    n_heads = width // HG_DK
    n_chunks = lp // HG_CHUNK
    group = max(g for g in range(1, HG_MAX_GROUP + 1) if n_chunks % g == 0)

    def col_spec(part):
        return pl.BlockSpec((1, lp, HG_DK), lambda b, j, part=part: (b, 0, part * n_heads + j))

    vec_spec = pl.BlockSpec((1, HG_DK), lambda b, j: (0, j))
    return pl.pallas_call(
        functools.partial(_hgrn_kernel, chunk=HG_CHUNK, group=group),
        grid=(nb, n_heads),
        in_specs=[col_spec(0), col_spec(1), col_spec(2), col_spec(3), vec_spec, vec_spec],
        out_specs=pl.BlockSpec((1, lp, HG_DK), lambda b, j: (b, 0, j)),
        out_shape=jax.ShapeDtypeStruct((nb, lp, width), BF16),
        compiler_params=pltpu.CompilerParams(
            dimension_semantics=("parallel", "parallel"), vmem_limit_bytes=VMEM_LIMIT),
        name="hgrn2_mixer",
    )(proj3, proj3, proj3, proj3, lb, norm_g)


def _block_tail_kernel(*refs, n_mix, alpha, ff_chunks):
    h_ref = refs[0]
    mix_refs = refs[1:1 + n_mix]
    w_refs = refs[1 + n_mix:1 + 2 * n_mix]
    (g1_ref, b1_ref, wup_ref, bup_ref, wdn_ref, bdn_ref, g2_ref, b2_ref, o_ref) = refs[1 + 2 * n_mix:]
    mix = jnp.dot(mix_refs[0][...], w_refs[0][...], preferred_element_type=F32)
    for m_ref, w_ref in zip(mix_refs[1:], w_refs[1:]):
        mix = mix + jnp.dot(m_ref[...], w_ref[...], preferred_element_type=F32)
    h1 = _layer_norm(alpha * h_ref[...] + mix, g1_ref[...], b1_ref[...])
    h1b = h1.astype(BF16)
    dff = wup_ref.shape[1]
    fc = dff // ff_chunks
    acc = alpha * h1 + bdn_ref[...]
    for c in range(ff_chunks):
        hid = jnp.dot(h1b, wup_ref[:, c * fc:(c + 1) * fc], preferred_element_type=F32)
        hid = jnp.maximum(hid + bup_ref[:, c * fc:(c + 1) * fc], 0.0)
        hid = (hid * hid).astype(BF16)
        acc = acc + jnp.dot(hid, wdn_ref[c * fc:(c + 1) * fc, :], preferred_element_type=F32)
    o_ref[...] = _layer_norm(acc, g2_ref[...], b2_ref[...])


def _block_tail(h, mixes, ws, g1, b1, wup, bup, wdn, bdn, g2, b2, alpha):
    rows, d = h.shape
    tm = _row_tile(rows, 512)
    row_spec = lambda n: pl.BlockSpec((tm, n), lambda i: (i, 0))
    consts = [g1, b1, wup, bup, wdn, bdn, g2, b2]
    return pl.pallas_call(
        functools.partial(_block_tail_kernel, n_mix=len(mixes), alpha=alpha, ff_chunks=4),
        grid=(rows // tm,),
        in_specs=([row_spec(d)] + [row_spec(m.shape[1]) for m in mixes]
                  + [_const_spec(w.shape) for w in ws] + [_const_spec(c.shape) for c in consts]),
        out_specs=row_spec(d),
        out_shape=jax.ShapeDtypeStruct((rows, d), F32),
        compiler_params=pltpu.CompilerParams(
            dimension_semantics=("parallel",), vmem_limit_bytes=VMEM_LIMIT),
        name="block_tail",
    )(h, *mixes, *ws, *consts)


def kernel(x, meta, w_in_ab, s5_lam_re, s5_lam_im, s5_log_dt, s5_b_re, s5_b_im, s5_c_re, s5_c_im, s5_d, s5_w_glu, s5_b_glu, w_out_ab, w_in_c, hgrn_gamma, hgrn_norm_g, w_out_c, ln_mix_g, ln_mix_b, mlp_w_up, mlp_b_up, mlp_w_down, mlp_b_down, ln_mlp_g, ln_mlp_b):
    nb, seq, d = x.shape
    depth = ln_mix_g.shape[0]
    assert depth == 2 and w_in_ab.shape[0] == 1 and w_in_c.shape[0] == 1
    alpha = (2.0 * depth) ** 0.25
    s5w = s5_w_glu.shape[1]
    sbw = (w_in_ab.shape[2] - s5w) // 3
    hgw = w_in_c.shape[2] // 4
    ltok = N_META + seq
    lp = -(-ltok // SEQ_ALIGN) * SEQ_ALIGN
    rows = nb * lp
    row2 = lambda a: a.reshape(1, -1).astype(F32)

    h = jnp.concatenate([jnp.broadcast_to(meta.astype(x.dtype)[None], (nb, N_META, d)), x,
                         jnp.zeros((nb, lp - ltok, d), x.dtype)], axis=1).reshape(rows, d)

    u, qkv = _inproj_ab(h, w_in_ab[0].astype(BF16), s5w, sbw)
    wb, wc, a_re, a_im = _s5_params(s5_lam_re[0], s5_lam_im[0], s5_log_dt[0], s5_b_re[0],
                                    s5_b_im[0], s5_c_re[0], s5_c_im[0])
    a_out = _s5(u.reshape(nb, lp, s5w), wb, a_re, a_im, wc, row2(s5_d[0]),
                s5_w_glu[0].astype(BF16), row2(s5_b_glu[0]))
    b_out = _stick_breaking(qkv.reshape(nb, lp, 3 * sbw), sbw)
    w_out = w_out_ab[0].astype(BF16)
    h = _block_tail(h, [a_out.reshape(rows, s5w), b_out.reshape(rows, sbw)],
                    [w_out[:s5w], w_out[s5w:]],
                    row2(ln_mix_g[0]), row2(ln_mix_b[0]), mlp_w_up[0].astype(BF16),
                    row2(mlp_b_up[0]), mlp_w_down[0].astype(BF16), row2(mlp_b_down[0]),
                    row2(ln_mlp_g[0]), row2(ln_mlp_b[0]), alpha)

    proj = _inproj_c(h, w_in_c[0].astype(BF16))
    p = jax.nn.softmax(hgrn_gamma.astype(F32), axis=0)
    lb = (jnp.cumsum(p, axis=0) - p[0])[1]
    c_out = _hgrn2(proj.reshape(nb, lp, 4 * hgw), row2(lb), row2(hgrn_norm_g[0]), hgw)
    h = _block_tail(h, [c_out.reshape(rows, hgw)], [w_out_c[0].astype(BF16)],
                    row2(ln_mix_g[1]), row2(ln_mix_b[1]), mlp_w_up[1].astype(BF16),
                    row2(mlp_b_up[1]), mlp_w_down[1].astype(BF16), row2(mlp_b_down[1]),
                    row2(ln_mlp_g[1]), row2(ln_mlp_b[1]), alpha)
    return h.reshape(nb, lp, d)[:, N_META:ltok, :]
```

```python
import functools
import math

import jax
import jax.numpy as jnp
from jax import lax
from jax.experimental import pallas as pl
from jax.experimental.pallas import tpu as pltpu

F32 = jnp.float32
BF16 = jnp.bfloat16

N_META = 16
S5_GROUP = 16
S5_STATE = 64
SB_HEAD_DIM = 64
HG_DK = 128
HG_CHUNK = 64
LN_EPS = 1e-5
RMS_EPS = 1e-6

LANES = 128
SEQ_ALIGN = 128
VMEM_LIMIT = 56 * 1024 * 1024


def _row_tile(rows, target):
    best = 8
    for t in range(8, min(rows, target) + 1, 8):
        if rows % t == 0:
            best = t
    return best


def _const_spec(shape):
    nd = len(shape)
    return pl.BlockSpec(shape, lambda *_: (0,) * nd, pipeline_mode=pl.Buffered(1))


def _nt_dot(a, b):
    return lax.dot_general(a, b, (((1,), (1,)), ((), ())), preferred_element_type=F32)


def _split_bf16(x):
    hi = x.astype(BF16)
    lo = (x - hi.astype(F32)).astype(BF16)
    return hi, lo


def _layer_norm(x, g, b):
    mu = jnp.mean(x, axis=-1, keepdims=True)
    xc = x - mu
    var = jnp.mean(xc * xc, axis=-1, keepdims=True)
    return xc * lax.rsqrt(var + LN_EPS) * g + b


def _sigmoid(x):
    return 1.0 / (1.0 + jnp.exp(-x))


def _inproj_ab_kernel(h_ref, w_ref, u_ref, qkv_ref, *, s5w, sbw):
    h = h_ref[...].astype(BF16)
    u_ref[...] = jnp.dot(h, w_ref[:, :s5w], preferred_element_type=F32)
    q = jnp.dot(h, w_ref[:, s5w:s5w + sbw], preferred_element_type=F32)
    qkv_ref[:, :sbw] = (q * (SB_HEAD_DIM ** -0.5)).astype(BF16)
    kv = jnp.dot(h, w_ref[:, s5w + sbw:], preferred_element_type=F32)
    qkv_ref[:, sbw:] = kv.astype(BF16)


def _inproj_ab(h, w, s5w, sbw):
    rows, d = h.shape
    tm = _row_tile(rows, 512)
    return pl.pallas_call(
        functools.partial(_inproj_ab_kernel, s5w=s5w, sbw=sbw),
        grid=(rows // tm,),
        in_specs=[pl.BlockSpec((tm, d), lambda i: (i, 0)), _const_spec(w.shape)],
        out_specs=[pl.BlockSpec((tm, s5w), lambda i: (i, 0)),
                   pl.BlockSpec((tm, 3 * sbw), lambda i: (i, 0))],
        out_shape=[jax.ShapeDtypeStruct((rows, s5w), F32),
                   jax.ShapeDtypeStruct((rows, 3 * sbw), BF16)],
        compiler_params=pltpu.CompilerParams(
            dimension_semantics=("parallel",), vmem_limit_bytes=VMEM_LIMIT),
        name="inproj_ab",
    )(h, w)


def _inproj_c_kernel(h_ref, w_ref, o_ref, *, n_split):
    h = h_ref[...].astype(BF16)
    n = w_ref.shape[1] // n_split
    for s in range(n_split):
        o_ref[:, s * n:(s + 1) * n] = jnp.dot(
            h, w_ref[:, s * n:(s + 1) * n], preferred_element_type=F32)


def _inproj_c(h, w):
    rows, d = h.shape
    n = w.shape[1]
    tm = _row_tile(rows, 512)
    return pl.pallas_call(
        functools.partial(_inproj_c_kernel, n_split=4),
        grid=(rows // tm,),
        in_specs=[pl.BlockSpec((tm, d), lambda i: (i, 0)), _const_spec(w.shape)],
        out_specs=pl.BlockSpec((tm, n), lambda i: (i, 0)),
        out_shape=jax.ShapeDtypeStruct((rows, n), F32),
        compiler_params=pltpu.CompilerParams(
            dimension_semantics=("parallel",), vmem_limit_bytes=VMEM_LIMIT),
        name="inproj_c",
    )(h, w)


S5_SCAN_UNROLL = 2


def _s5_kernel(u_ref, perm_ref, wb_ref, are_ref, aim_ref, wc_ref, d_ref, wglu_ref, bglu_ref,
               o_ref, bu_ref, sre_ref, sim_ref, ytm_ref, y_ref, *, nb, tc, slab_group):
    width = u_ref.shape[2]
    sc = are_ref.shape[1]
    nq = width // LANES
    per_q = sc // nq

    @pl.when(pl.program_id(0) == 0)
    def _():
        sre_ref[...] = jnp.zeros_like(sre_ref)
        sim_ref[...] = jnp.zeros_like(sim_ref)

    n_slabs = sc // LANES
    slabs_q = per_q // LANES
    u = u_ref[...].reshape(nb * tc, width)
    u_tm = jnp.dot(perm_ref[...], u.astype(BF16), preferred_element_type=F32).astype(BF16)
    for q in range(nq):
        r = jnp.dot(u_tm[:, q * LANES:(q + 1) * LANES], wb_ref[q], preferred_element_type=F32)
        for j in range(slabs_q):
            bu_ref[q * slabs_q + j] = r[:, j * LANES:(j + 1) * LANES]
            bu_ref[n_slabs + q * slabs_q + j] = r[:, per_q + j * LANES:per_q + (j + 1) * LANES]

    for s0 in range(0, n_slabs, slab_group):
        slabs = list(range(s0, s0 + slab_group))
        a_re = [jnp.broadcast_to(are_ref[:, s * LANES:(s + 1) * LANES], (nb, LANES)) for s in slabs]
        a_im = [jnp.broadcast_to(aim_ref[:, s * LANES:(s + 1) * LANES], (nb, LANES)) for s in slabs]
        init = []
        for s in slabs:
            init += [sre_ref[:, s * LANES:(s + 1) * LANES], sim_ref[:, s * LANES:(s + 1) * LANES]]

        def body(t, carry, slabs=slabs, a_re=a_re, a_im=a_im):
            new = []
            rows = pl.ds(pl.multiple_of(t * nb, nb), nb)
            for i, s in enumerate(slabs):
                sr, si = carry[2 * i], carry[2 * i + 1]
                nr = a_re[i] * sr - a_im[i] * si + bu_ref[s, rows, :]
                ni = a_re[i] * si + a_im[i] * sr + bu_ref[n_slabs + s, rows, :]
                bu_ref[s, rows, :] = nr
                bu_ref[n_slabs + s, rows, :] = ni
                new += [nr, ni]
            return tuple(new)

        fin = lax.fori_loop(0, tc, body, tuple(init), unroll=S5_SCAN_UNROLL)
        for i, s in enumerate(slabs):
            sre_ref[:, s * LANES:(s + 1) * LANES] = fin[2 * i]
            sim_ref[:, s * LANES:(s + 1) * LANES] = fin[2 * i + 1]

    for q in range(nq):
        s_re = jnp.concatenate(
            [bu_ref[q * slabs_q + j].astype(BF16) for j in range(slabs_q)], axis=1)
        s_im = jnp.concatenate(
            [bu_ref[n_slabs + q * slabs_q + j].astype(BF16) for j in range(slabs_q)], axis=1)
        ytm_ref[q] = (jnp.dot(s_re, wc_ref[q, :per_q, :], preferred_element_type=F32)
                      + jnp.dot(s_im, wc_ref[q, per_q:, :], preferred_element_type=F32))
    for q in range(nq):
        for b in range(nb):
            y_ref[b * tc:(b + 1) * tc, q * LANES:(q + 1) * LANES] = (
                ytm_ref[q, pl.ds(b, tc, stride=nb), :])
    y = y_ref[...] + d_ref[...] * u
    y = 0.5 * y * (1.0 + jnp.tanh(math.sqrt(2.0 / math.pi) * (y + 0.044715 * (y * y * y))))
    gate =jnp.dot(y.astype(BF16), wglu_ref[...], preferred_element_type=F32) + bglu_ref[...]
    o_ref[...] = (y * _sigmoid(gate)).reshape(nb, tc, width).astype(o_ref.dtype)


def _s5_params(lam_re, lam_im, log_dt, b_re, b_im, c_re, c_im):
    g, p = lam_re.shape
    lr, li = lam_re.astype(F32), lam_im.astype(F32)
    dt = jnp.exp(log_dt.astype(F32))[:, None]
    mag = jnp.exp(lr * dt)
    a_re = mag * jnp.cos(li * dt)
    a_im = mag * jnp.sin(li * dt)
    den = lr * lr + li * li
    c_re_ = ((a_re - 1.0) * lr + a_im * li) / den
    c_im_ = (a_im * lr - (a_re - 1.0) * li) / den
    bb_re = c_re_[:, :, None] * b_re.astype(F32) - c_im_[:, :, None] * b_im.astype(F32)
    bb_im = c_re_[:, :, None] * b_im.astype(F32) + c_im_[:, :, None] * b_re.astype(F32)
    gq = LANES // S5_GROUP
    nq = g // gq
    eye = jnp.eye(gq, dtype=F32)

    def b_slab(x):
        x = x.reshape(nq, gq, p, S5_GROUP)
        return jnp.einsum('qgph,gk->qghkp', x, eye).reshape(nq, gq * S5_GROUP, gq * p)

    def c_slab(x):
        x = x.reshape(nq, gq, S5_GROUP, p)
        return jnp.einsum('qghp,gk->qgpkh', x, eye).reshape(nq, gq * p, gq * S5_GROUP)

    wb = jnp.concatenate([b_slab(bb_re), b_slab(bb_im)], axis=2)
    wc = jnp.concatenate([c_slab(c_re.astype(F32)), c_slab(-c_im.astype(F32))], axis=1)
    return wb.astype(BF16), wc.astype(BF16), a_re.reshape(1, g * p), a_im.reshape(1, g * p)


def _s5(u3, wb, a_re, a_im, wc, d, wglu, bglu):
    nb, lp, width = u3.shape
    sc = a_re.shape[1]
    tc = 64
    assert lp % tc == 0
    rows = nb * tc
    r = jnp.arange(rows)
    perm = (jnp.arange(rows)[None, :] == ((r % nb) * tc + r // nb)[:, None]).astype(BF16)
    return pl.pallas_call(
        functools.partial(_s5_kernel, nb=nb, tc=tc, slab_group=4),
        grid=(lp // tc,),
        in_specs=[pl.BlockSpec((nb, tc, width), lambda t: (0, t, 0)), _const_spec(perm.shape),
                  _const_spec(wb.shape), _const_spec(a_re.shape), _const_spec(a_im.shape),
                  _const_spec(wc.shape), _const_spec(d.shape), _const_spec(wglu.shape),
                  _const_spec(bglu.shape)],
        out_specs=pl.BlockSpec((nb, tc, width), lambda t: (0, t, 0)),
        out_shape=jax.ShapeDtypeStruct((nb, lp, width), BF16),
        scratch_shapes=[pltpu.VMEM((2 * sc // LANES, rows, LANES), F32),
                        pltpu.VMEM((nb, sc), F32), pltpu.VMEM((nb, sc), F32),
                        pltpu.VMEM((width // LANES, rows, LANES), F32),
                        pltpu.VMEM((rows, width), F32)],
        compiler_params=pltpu.CompilerParams(
            dimension_semantics=("arbitrary",), vmem_limit_bytes=VMEM_LIMIT),
        name="s5_mixer",
    )(u3, perm, wb, a_re, a_im, wc, d, wglu, bglu)


MASKED_LOG = -1e30
SB_UNROLL = 4
SB_LEAD = 3
SB_TRAIL = SB_LEAD + SB_UNROLL - 1


def _sb_kernel(tbl_ref, q_ref, k_ref, v_ref, o_ref, q_scr, km_ref, vm_ref, carry_scr, acc_scr,
               tri_scr, *bufs, tq, kb, n_diag, n_off):
    lp = k_ref.shape[1]
    n_pairs = k_ref.shape[2] // LANES
    n_q = km_ref.shape[0] // n_pairs
    n_heads = LANES // SB_HEAD_DIM
    reps = kb // LANES

    carry_scr[...] = jnp.zeros_like(carry_scr)
    acc_scr[...] = jnp.zeros_like(acc_scr)
    lane = lax.broadcasted_iota(jnp.int32, (kb, LANES), 1)
    for pair in range(n_pairs):
        cols = slice(pair * LANES, (pair + 1) * LANES)
        q0 = pair * n_q * tq
        q_scr[q0:q0 + lp, :] = q_ref[0, :, cols]
        if n_q * tq > lp:
            q_scr[q0 + lp:q0 + n_q * tq, :] = jnp.zeros((n_q * tq - lp, LANES), q_scr.dtype)
        for m in range(n_q):
            valid = min(kb, lp - m * kb)
            k_blk = k_ref[0, m * kb:m * kb + valid, cols]
            v_blk = v_ref[0, m * kb:m * kb + valid, cols]
            if valid < kb:
                pad = jnp.zeros((kb - valid, LANES), k_blk.dtype)
                k_blk = jnp.concatenate([k_blk, pad], axis=0)
                v_blk = jnp.concatenate([v_blk, pad], axis=0)
            for h in range(n_heads):
                in_head = (lane >= h * SB_HEAD_DIM) & (lane < (h + 1) * SB_HEAD_DIM)
                km_ref[pair * n_q + m, h] = jnp.where(in_head, k_blk, jnp.zeros_like(k_blk))
                vm_ref[pair * n_q + m, h] = jnp.where(in_head, v_blk, jnp.zeros_like(v_blk))

    row = lax.broadcasted_iota(jnp.int32, (kb, kb), 0)
    col = lax.broadcasted_iota(jnp.int32, (kb, kb), 1)
    tri = jnp.where(row > col, 1.0, 0.0).astype(BF16)
    below_diag = lax.broadcasted_iota(jnp.int32, (tq, kb), 1) < lax.broadcasted_iota(
        jnp.int32, (tq, kb), 0)

    def item(j):
        return tbl_ref[3 * j], tbl_ref[3 * j + 1], tbl_ref[3 * j + 2]

    z_buf, hl_buf, base_buf, rs_buf, w_buf = (bufs[0:2], bufs[2:4], bufs[4:6], bufs[6:8],
                                              bufs[8:10])
    tri_scr[...] = tri
    for ref in (z_buf[0], hl_buf[1], base_buf[1], rs_buf[1], w_buf[0]):
        ref[...] = jnp.zeros_like(ref)

    def scores(j, z_ref):
        qt, m, _ = item(j)
        q = q_scr[pl.ds(pl.multiple_of(qt * tq, tq), tq), :]
        for h in range(n_heads):
            z_ref[h] = _nt_dot(q, km_ref[m, h])

    def log_terms(h, z_ref, hl_ref, base_ref, rs_ref, diagonal):
        z = z_ref[h]
        nlk = jnp.maximum(z, 0.0) + jnp.log(1.0 + jnp.exp(-jnp.abs(z)))
        base = z - nlk
        if diagonal:
            nlk = jnp.where(below_diag, nlk, 0.0)
            base = jnp.where(below_diag, base, MASKED_LOG)
        hl_ref[h] = nlk.astype(BF16)
        base_ref[h] = base
        rs_ref[h] = jnp.broadcast_to(jnp.sum(nlk, axis=1, keepdims=True), (tq, LANES))

    def suffix_sum(h, j, hl_ref):
        _, _, slot = item(j)
        return (jnp.dot(hl_ref[h], tri_scr[...], preferred_element_type=F32)
                + jnp.concatenate([carry_scr[slot, h]] * reps, axis=1))

    def weights(h, j, later, base_ref, rs_ref, w_ref):
        _, _, slot = item(j)
        w_ref[h] = jnp.exp(base_ref[h] - later).astype(BF16)
        carry_scr[slot, h] = carry_scr[slot, h] + rs_ref[h]

    def weighted_values(j, w_ref):
        _, m, slot = item(j)
        pv = jnp.dot(w_ref[0], vm_ref[m, 0], preferred_element_type=F32)
        for h in range(1, n_heads):
            pv = pv + jnp.dot(w_ref[h], vm_ref[m, h], preferred_element_type=F32)
        acc_scr[slot] = acc_scr[slot] + pv

    def run(first, n_real, diagonal):
        n_iter = -(-(n_real + SB_LEAD) // SB_UNROLL) * SB_UNROLL

        def step(i, p):
            j = first + i
            later = []
            for h in range(n_heads):
                log_terms(h, z_buf[p], hl_buf[p], base_buf[p], rs_buf[p], diagonal)
                if h == 0:
                    weighted_values(j - 2, w_buf[p])
                later.append(suffix_sum(h, j - 1, hl_buf[1 - p]))
            for h in range(n_heads):
                weights(h, j - 1, later[h], base_buf[1 - p], rs_buf[1 - p], w_buf[1 - p])
                if h == 0:
                    scores(j + 1, z_buf[1 - p])

        def body(it, c):
            for k in range(SB_UNROLL):
                step(SB_UNROLL * it + k, k % 2)
            return c

        lax.fori_loop(0, n_iter // SB_UNROLL, body, 0)

    run(SB_LEAD - 1, n_diag, True)
    run(n_diag + SB_LEAD + SB_TRAIL + SB_LEAD - 1, n_off, False)
    for pair in range(n_pairs):
        o_ref[0, :, pair * LANES:(pair + 1) * LANES] = acc_scr[
            pair * n_q:(pair + 1) * n_q].reshape(n_q * tq, LANES)[:lp].astype(o_ref.dtype)


def _sb_items(n_q, n_pairs):
    dummy = [(0, 0, n_pairs * n_q)]
    base = [p * n_q for p in range(n_pairs)]
    diag = [(o + t, o + t, o + t) for o in base for t in range(n_q)]
    off = [(o + t, o + m, o + t) for o in base for t in range(n_q) for m in range(t - 1, -1, -1)]
    flat = (SB_LEAD * dummy + diag + SB_TRAIL * dummy + SB_LEAD * dummy + off + SB_TRAIL * dummy)
    return len(diag), len(off), jnp.asarray([x for e in flat for x in e], jnp.int32)


SB_PAIRS_PER_STEP = 4


def _stick_breaking(qkv3, sbw):
    nb, lp, _ = qkv3.shape
    tq = kb = 2 * SEQ_ALIGN
    n_heads = LANES // SB_HEAD_DIM
    pps = math.gcd(SB_PAIRS_PER_STEP, sbw // LANES)
    n_steps = sbw // (pps * LANES)
    bw = pps * LANES
    n_q = -(-lp // tq)
    n_diag, n_off, table = _sb_items(n_q, pps)
    masked_kv = pltpu.VMEM((pps * n_q, n_heads, kb, LANES), BF16)
    grid_spec = pltpu.PrefetchScalarGridSpec(
        num_scalar_prefetch=1,
        grid=(nb, n_steps),
        in_specs=[pl.BlockSpec((1, lp, bw), lambda b, p, t: (b, 0, p)),
                  pl.BlockSpec((1, lp, bw), lambda b, p, t: (b, 0, n_steps + p)),
                  pl.BlockSpec((1, lp, bw), lambda b, p, t: (b, 0, 2 * n_steps + p))],
        out_specs=pl.BlockSpec((1, lp, bw), lambda b, p, t: (b, 0, p)),
        scratch_shapes=[pltpu.VMEM((pps * n_q * tq, LANES), BF16), masked_kv, masked_kv,
                        pltpu.VMEM((pps * n_q + 1, n_heads, tq, LANES), F32),
                        pltpu.VMEM((pps * n_q + 1, tq, LANES), F32),
                        pltpu.VMEM((kb, kb), BF16)]
        + 2 * [pltpu.VMEM((n_heads, tq, kb), F32)]
        + 2 * [pltpu.VMEM((n_heads, tq, kb), BF16)]
        + 2 * [pltpu.VMEM((n_heads, tq, kb), F32)]
        + 2 * [pltpu.VMEM((n_heads, tq, LANES), F32)]
        + 2 * [pltpu.VMEM((n_heads, tq, kb), BF16)])
    return pl.pallas_call(
        functools.partial(_sb_kernel, tq=tq, kb=kb, n_diag=n_diag, n_off=n_off),
        grid_spec=grid_spec,
        out_shape=jax.ShapeDtypeStruct((nb, lp, sbw), BF16),
        compiler_params=pltpu.CompilerParams(
            dimension_semantics=("parallel", "parallel"), vmem_limit_bytes=VMEM_LIMIT),
        name="stick_breaking",
    )(table, qkv3, qkv3, qkv3)


def _hgrn_kernel(q_ref, f_ref, v_ref, g_ref, lb_ref, ng_ref, o_ref, *, chunk, group):
    lp = q_ref.shape[1]
    rows = chunk * group
    row = lax.broadcasted_iota(jnp.int32, (chunk, chunk), 0)
    col = lax.broadcasted_iota(jnp.int32, (chunk, chunk), 1)
    causal = col <= row
    tri = jnp.where(causal, 1.0, 0.0).astype(BF16)
    tri2 = jnp.concatenate([tri, tri], axis=1)
    lb = lb_ref[...]

    def body(s, st):
        sl = pl.ds(pl.multiple_of(s * rows, rows), rows)
        f = lb + (1.0 - lb) * _sigmoid(f_ref[0, sl, :])
        kk = 1.0 - f
        hi, lo = _split_bf16(jnp.log(f))
        cs = [slice(c * chunk, (c + 1) * chunk) for c in range(group)]
        bcum = jnp.concatenate(
            [jnp.dot(tri2, jnp.concatenate([hi[r], lo[r]], axis=0), preferred_element_type=F32)
             for r in cs], axis=0)
        q_dec = (q_ref[0, sl, :] * jnp.exp(bcum)).astype(BF16)
        k_inv_f = kk * jnp.exp(-bcum)
        k_inv = k_inv_f.astype(BF16)
        v = v_ref[0, sl, :].astype(BF16)
        outs = []
        for r in cs:
            e_last = jnp.exp(bcum[r.stop - 1:r.stop, :])
            k_end = (k_inv_f[r] * e_last).astype(BF16)
            scores = jnp.where(causal, _nt_dot(q_dec[r], k_inv[r]), 0.0)
            outs.append(jnp.dot(scores.astype(BF16), v[r], preferred_element_type=F32)
                        + _nt_dot(q_dec[r], st.astype(BF16)))
            st = st * e_last + lax.dot_general(v[r], k_end, (((0,), (0,)), ((), ())),
                                               preferred_element_type=F32)
        o = jnp.concatenate(outs, axis=0)
        o = o * lax.rsqrt(jnp.mean(o * o, axis=-1, keepdims=True) + RMS_EPS)
        g = g_ref[0, sl, :]
        o_ref[0, sl, :] = (o * ng_ref[...] * (g * _sigmoid(g))).astype(o_ref.dtype)
        return st

    lax.fori_loop(0, lp // rows, body, jnp.zeros((HG_DK, HG_DK), F32))


HG_MAX_GROUP = 17


def _hgrn2(proj3, lb, norm_g, width):
    nb, lp, _ = proj3.shape
    n_heads = width // HG_DK
    n_chunks = lp // HG_CHUNK
    group = max(g for g in range(1, HG_MAX_GROUP + 1) if n_chunks % g == 0)

    def col_spec(part):
        return pl.BlockSpec((1, lp, HG_DK), lambda b, j, part=part: (b, 0, part * n_heads + j))

    vec_spec = pl.BlockSpec((1, HG_DK), lambda b, j: (0, j))
    return pl.pallas_call(
        functools.partial(_hgrn_kernel, chunk=HG_CHUNK, group=group),
        grid=(nb, n_heads),
        in_specs=[col_spec(0), col_spec(1), col_spec(2), col_spec(3), vec_spec, vec_spec],
        out_specs=pl.BlockSpec((1, lp, HG_DK), lambda b, j: (b, 0, j)),
        out_shape=jax.ShapeDtypeStruct((nb, lp, width), BF16),
        compiler_params=pltpu.CompilerParams(
            dimension_semantics=("parallel", "parallel"), vmem_limit_bytes=VMEM_LIMIT),
        name="hgrn2_mixer",
    )(proj3, proj3, proj3, proj3, lb, norm_g)


def _block_tail_kernel(*refs, n_mix, alpha, ff_chunks):
    h_ref = refs[0]
    mix_refs = refs[1:1 + n_mix]
    w_refs = refs[1 + n_mix:1 + 2 * n_mix]
    (g1_ref, b1_ref, wup_ref, bup_ref, wdn_ref, bdn_ref, g2_ref, b2_ref, o_ref) = refs[1 + 2 * n_mix:]
    mix = jnp.dot(mix_refs[0][...], w_refs[0][...], preferred_element_type=F32)
    for m_ref, w_ref in zip(mix_refs[1:], w_refs[1:]):
        mix = mix + jnp.dot(m_ref[...], w_ref[...], preferred_element_type=F32)
    h1 = _layer_norm(alpha * h_ref[...] + mix, g1_ref[...], b1_ref[...])
    h1b = h1.astype(BF16)
    dff = wup_ref.shape[1]
    fc = dff // ff_chunks
    acc = alpha * h1 + bdn_ref[...]
    for c in range(ff_chunks):
        hid = jnp.dot(h1b, wup_ref[:, c * fc:(c + 1) * fc], preferred_element_type=F32)
        hid = jnp.maximum(hid + bup_ref[:, c * fc:(c + 1) * fc], 0.0)
        hid = (hid * hid).astype(BF16)
        acc = acc + jnp.dot(hid, wdn_ref[c * fc:(c + 1) * fc, :], preferred_element_type=F32)
    o_ref[...] = _layer_norm(acc, g2_ref[...], b2_ref[...])


def _block_tail(h, mixes, ws, g1, b1, wup, bup, wdn, bdn, g2, b2, alpha, window=None):
    rows, d = h.shape
    consts = [g1, b1, wup, bup, wdn, bdn, g2, b2]
    if window is None:
        tm = _row_tile(rows, 512)
        grid = (rows // tm,)
        row_spec = lambda n: pl.BlockSpec((tm, n), lambda i: (i, 0))
        out_spec, out_rows = row_spec(d), rows
    else:
        nb, lp, start, count = window
        tm = _row_tile(count, 512)
        per_seq = count // tm
        grid = (nb, per_seq)
        align = math.gcd(lp, start, tm)
        row_spec = lambda n: pl.BlockSpec(
            (pl.Element(tm), pl.Element(n)),
            lambda b, j: (pl.multiple_of(b * lp + start + j * tm, align), 0))
        out_spec = pl.BlockSpec((tm, d), lambda b, j: (b * per_seq + j, 0))
        out_rows = nb * count
    return pl.pallas_call(
        functools.partial(_block_tail_kernel, n_mix=len(mixes), alpha=alpha, ff_chunks=4),
        grid=grid,
        in_specs=([row_spec(d)] + [row_spec(m.shape[1]) for m in mixes]
                  + [_const_spec(w.shape) for w in ws] + [_const_spec(c.shape) for c in consts]),
        out_specs=out_spec,
        out_shape=jax.ShapeDtypeStruct((out_rows, d), F32),
        compiler_params=pltpu.CompilerParams(
            dimension_semantics=("parallel",) * len(grid), vmem_limit_bytes=VMEM_LIMIT),
        name="block_tail",
    )(h, *mixes, *ws, *consts)


def kernel(x, meta, w_in_ab, s5_lam_re, s5_lam_im, s5_log_dt, s5_b_re, s5_b_im, s5_c_re, s5_c_im, s5_d, s5_w_glu, s5_b_glu, w_out_ab, w_in_c, hgrn_gamma, hgrn_norm_g, w_out_c, ln_mix_g, ln_mix_b, mlp_w_up, mlp_b_up, mlp_w_down, mlp_b_down, ln_mlp_g, ln_mlp_b):
    nb, seq, d = x.shape
    depth = ln_mix_g.shape[0]
    assert depth == 2 and w_in_ab.shape[0] == 1 and w_in_c.shape[0] == 1
    alpha = (2.0 * depth) ** 0.25
    s5w = s5_w_glu.shape[1]
    sbw = (w_in_ab.shape[2] - s5w) // 3
    hgw = w_in_c.shape[2] // 4
    ltok = N_META + seq
    lp = -(-ltok // SEQ_ALIGN) * SEQ_ALIGN
    rows = nb * lp
    row2 = lambda a: a.reshape(1, -1).astype(F32)

    h = jnp.concatenate([jnp.broadcast_to(meta.astype(x.dtype)[None], (nb, N_META, d)), x,
                         jnp.zeros((nb, lp - ltok, d), x.dtype)], axis=1).reshape(rows, d)

    u, qkv = _inproj_ab(h, w_in_ab[0].astype(BF16), s5w, sbw)
    wb, wc, a_re, a_im = _s5_params(s5_lam_re[0], s5_lam_im[0], s5_log_dt[0], s5_b_re[0],
                                    s5_b_im[0], s5_c_re[0], s5_c_im[0])
    a_out = _s5(u.reshape(nb, lp, s5w), wb, a_re, a_im, wc, row2(s5_d[0]),
                s5_w_glu[0].astype(BF16), row2(s5_b_glu[0]))
    b_out = _stick_breaking(qkv.reshape(nb, lp, 3 * sbw), sbw)
    w_out = w_out_ab[0].astype(BF16)
    h = _block_tail(h, [a_out.reshape(rows, s5w), b_out.reshape(rows, sbw)],
                    [w_out[:s5w], w_out[s5w:]],
                    row2(ln_mix_g[0]), row2(ln_mix_b[0]), mlp_w_up[0].astype(BF16),
                    row2(mlp_b_up[0]), mlp_w_down[0].astype(BF16), row2(mlp_b_down[0]),
                    row2(ln_mlp_g[0]), row2(ln_mlp_b[0]), alpha)

    proj = _inproj_c(h, w_in_c[0].astype(BF16))
    p = jax.nn.softmax(hgrn_gamma.astype(F32), axis=0)
    lb = (jnp.cumsum(p, axis=0) - p[0])[1]
    c_out = _hgrn2(proj.reshape(nb, lp, 4 * hgw), row2(lb), row2(hgrn_norm_g[0]), hgw)
    h = _block_tail(h, [c_out.reshape(rows, hgw)], [w_out_c[0].astype(BF16)],
                    row2(ln_mix_g[1]), row2(ln_mix_b[1]), mlp_w_up[1].astype(BF16),
                    row2(mlp_b_up[1]), mlp_w_down[1].astype(BF16), row2(mlp_b_down[1]),
                    row2(ln_mlp_g[1]), row2(ln_mlp_b[1]), alpha, window=(nb, lp, N_META, seq))
    return h.reshape(nb, seq, d)
```

```python
import functools
import math

import jax
import jax.numpy as jnp
from jax import lax
from jax.experimental import pallas as pl
from jax.experimental.pallas import tpu as pltpu

F32 = jnp.float32
BF16 = jnp.bfloat16

N_META = 16
S5_GROUP = 16
S5_STATE = 64
SB_HEAD_DIM = 64
HG_DK = 128
HG_CHUNK = 64
LN_EPS = 1e-5
RMS_EPS = 1e-6

LANES = 128
SEQ_ALIGN = 128
SB_TILE = 256
VMEM_LIMIT = 56 * 1024 * 1024


def _row_tile(rows, target):
    best = 8
    for t in range(8, min(rows, target) + 1, 8):
        if rows % t == 0:
            best = t
    return best


def _const_spec(shape):
    nd = len(shape)
    return pl.BlockSpec(shape, lambda *_: (0,) * nd, pipeline_mode=pl.Buffered(1))


def _nt_dot(a, b):
    return lax.dot_general(a, b, (((1,), (1,)), ((), ())), preferred_element_type=F32)


def _split_bf16(x):
    hi = x.astype(BF16)
    lo = (x - hi.astype(F32)).astype(BF16)
    return hi, lo


def _layer_norm(x, g, b):
    mu = jnp.mean(x, axis=-1, keepdims=True)
    xc = x - mu
    var = jnp.mean(xc * xc, axis=-1, keepdims=True)
    return xc * lax.rsqrt(var + LN_EPS) * g + b


def _sigmoid(x):
    return 1.0 / (1.0 + jnp.exp(-x))


def _inproj_ab_kernel(h_ref, w_ref, u_ref, qkv_ref, *, s5w, sbw):
    h = h_ref[...].astype(BF16)
    u_ref[...] = jnp.dot(h, w_ref[:, :s5w], preferred_element_type=F32)
    q = jnp.dot(h, w_ref[:, s5w:s5w + sbw], preferred_element_type=F32)
    qkv_ref[:, :sbw] = (q * (SB_HEAD_DIM ** -0.5)).astype(BF16)
    kv = jnp.dot(h, w_ref[:, s5w + sbw:], preferred_element_type=F32)
    qkv_ref[:, sbw:] = kv.astype(BF16)


def _inproj_ab(h, w, s5w, sbw):
    rows, d = h.shape
    tm = _row_tile(rows, 512)
    return pl.pallas_call(
        functools.partial(_inproj_ab_kernel, s5w=s5w, sbw=sbw),
        grid=(rows // tm,),
        in_specs=[pl.BlockSpec((tm, d), lambda i: (i, 0)), _const_spec(w.shape)],
        out_specs=[pl.BlockSpec((tm, s5w), lambda i: (i, 0)),
                   pl.BlockSpec((tm, 3 * sbw), lambda i: (i, 0))],
        out_shape=[jax.ShapeDtypeStruct((rows, s5w), F32),
                   jax.ShapeDtypeStruct((rows, 3 * sbw), BF16)],
        compiler_params=pltpu.CompilerParams(
            dimension_semantics=("parallel",), vmem_limit_bytes=VMEM_LIMIT),
        name="inproj_ab",
    )(h, w)


def _inproj_c_kernel(h_ref, w_ref, o_ref, *, n_split):
    h = h_ref[...].astype(BF16)
    n = w_ref.shape[1] // n_split
    for s in range(n_split):
        o_ref[:, s * n:(s + 1) * n] = jnp.dot(
            h, w_ref[:, s * n:(s + 1) * n], preferred_element_type=F32)


def _inproj_c(h, w):
    rows, d = h.shape
    n = w.shape[1]
    tm = _row_tile(rows, 512)
    return pl.pallas_call(
        functools.partial(_inproj_c_kernel, n_split=4),
        grid=(rows // tm,),
        in_specs=[pl.BlockSpec((tm, d), lambda i: (i, 0)), _const_spec(w.shape)],
        out_specs=pl.BlockSpec((tm, n), lambda i: (i, 0)),
        out_shape=jax.ShapeDtypeStruct((rows, n), F32),
        compiler_params=pltpu.CompilerParams(
            dimension_semantics=("parallel",), vmem_limit_bytes=VMEM_LIMIT),
        name="inproj_c",
    )(h, w)


S5_SCAN_UNROLL = 2


def _s5_kernel(u_ref, perm_ref, wb_ref, are_ref, aim_ref, wc_ref, d_ref, wglu_ref, bglu_ref,
               o_ref, bu_ref, sre_ref, sim_ref, ytm_ref, y_ref, *, nb, tc, slab_group):
    width = u_ref.shape[2]
    sc = are_ref.shape[1]
    nq = width // LANES
    per_q = sc // nq

    @pl.when(pl.program_id(0) == 0)
    def _():
        sre_ref[...] = jnp.zeros_like(sre_ref)
        sim_ref[...] = jnp.zeros_like(sim_ref)

    n_slabs = sc // LANES
    slabs_q = per_q // LANES
    u = u_ref[...].reshape(nb * tc, width)
    u_tm = jnp.dot(perm_ref[...], u.astype(BF16), preferred_element_type=F32).astype(BF16)
    for q in range(nq):
        r = jnp.dot(u_tm[:, q * LANES:(q + 1) * LANES], wb_ref[q], preferred_element_type=F32)
        for j in range(slabs_q):
            bu_ref[q * slabs_q + j] = r[:, j * LANES:(j + 1) * LANES]
            bu_ref[n_slabs + q * slabs_q + j] = r[:, per_q + j * LANES:per_q + (j + 1) * LANES]

    for s0 in range(0, n_slabs, slab_group):
        slabs = list(range(s0, s0 + slab_group))
        a_re = [jnp.broadcast_to(are_ref[:, s * LANES:(s + 1) * LANES], (nb, LANES)) for s in slabs]
        a_im = [jnp.broadcast_to(aim_ref[:, s * LANES:(s + 1) * LANES], (nb, LANES)) for s in slabs]
        init = []
        for s in slabs:
            init += [sre_ref[:, s * LANES:(s + 1) * LANES], sim_ref[:, s * LANES:(s + 1) * LANES]]

        def body(t, carry, slabs=slabs, a_re=a_re, a_im=a_im):
            new = []
            rows = pl.ds(pl.multiple_of(t * nb, nb), nb)
            for i, s in enumerate(slabs):
                sr, si = carry[2 * i], carry[2 * i + 1]
                nr = a_re[i] * sr - a_im[i] * si + bu_ref[s, rows, :]
                ni = a_re[i] * si + a_im[i] * sr + bu_ref[n_slabs + s, rows, :]
                bu_ref[s, rows, :] = nr
                bu_ref[n_slabs + s, rows, :] = ni
                new += [nr, ni]
            return tuple(new)

        fin = lax.fori_loop(0, tc, body, tuple(init), unroll=S5_SCAN_UNROLL)
        for i, s in enumerate(slabs):
            sre_ref[:, s * LANES:(s + 1) * LANES] = fin[2 * i]
            sim_ref[:, s * LANES:(s + 1) * LANES] = fin[2 * i + 1]

    for q in range(nq):
        s_re = jnp.concatenate(
            [bu_ref[q * slabs_q + j].astype(BF16) for j in range(slabs_q)], axis=1)
        s_im = jnp.concatenate(
            [bu_ref[n_slabs + q * slabs_q + j].astype(BF16) for j in range(slabs_q)], axis=1)
        ytm_ref[q] = (jnp.dot(s_re, wc_ref[q, :per_q, :], preferred_element_type=F32)
                      + jnp.dot(s_im, wc_ref[q, per_q:, :], preferred_element_type=F32))
    for q in range(nq):
        for b in range(nb):
            y_ref[b * tc:(b + 1) * tc, q * LANES:(q + 1) * LANES] = (
                ytm_ref[q, pl.ds(b, tc, stride=nb), :])
    y = y_ref[...] + d_ref[...] * u
    y = 0.5 * y * (1.0 + jnp.tanh(math.sqrt(2.0 / math.pi) * (y + 0.044715 * (y * y * y))))
    gate =jnp.dot(y.astype(BF16), wglu_ref[...], preferred_element_type=F32) + bglu_ref[...]
    o_ref[...] = (y * _sigmoid(gate)).reshape(nb, tc, width).astype(o_ref.dtype)


def _s5_params(lam_re, lam_im, log_dt, b_re, b_im, c_re, c_im):
    g, p = lam_re.shape
    lr, li = lam_re.astype(F32), lam_im.astype(F32)
    dt = jnp.exp(log_dt.astype(F32))[:, None]
    mag = jnp.exp(lr * dt)
    a_re = mag * jnp.cos(li * dt)
    a_im = mag * jnp.sin(li * dt)
    den = lr * lr + li * li
    c_re_ = ((a_re - 1.0) * lr + a_im * li) / den
    c_im_ = (a_im * lr - (a_re - 1.0) * li) / den
    bb_re = c_re_[:, :, None] * b_re.astype(F32) - c_im_[:, :, None] * b_im.astype(F32)
    bb_im = c_re_[:, :, None] * b_im.astype(F32) + c_im_[:, :, None] * b_re.astype(F32)
    gq = LANES // S5_GROUP
    nq = g // gq
    eye = jnp.eye(gq, dtype=F32)

    def b_slab(x):
        x = x.reshape(nq, gq, p, S5_GROUP)
        return jnp.einsum('qgph,gk->qghkp', x, eye).reshape(nq, gq * S5_GROUP, gq * p)

    def c_slab(x):
        x = x.reshape(nq, gq, S5_GROUP, p)
        return jnp.einsum('qghp,gk->qgpkh', x, eye).reshape(nq, gq * p, gq * S5_GROUP)

    wb = jnp.concatenate([b_slab(bb_re), b_slab(bb_im)], axis=2)
    wc = jnp.concatenate([c_slab(c_re.astype(F32)), c_slab(-c_im.astype(F32))], axis=1)
    return wb.astype(BF16), wc.astype(BF16), a_re.reshape(1, g * p), a_im.reshape(1, g * p)


def _s5(u3, wb, a_re, a_im, wc, d, wglu, bglu):
    nb, lp, width = u3.shape
    sc = a_re.shape[1]
    tc = 64
    assert lp % tc == 0
    rows = nb * tc
    r = jnp.arange(rows)
    perm = (jnp.arange(rows)[None, :] == ((r % nb) * tc + r // nb)[:, None]).astype(BF16)
    return pl.pallas_call(
        functools.partial(_s5_kernel, nb=nb, tc=tc, slab_group=4),
        grid=(lp // tc,),
        in_specs=[pl.BlockSpec((nb, tc, width), lambda t: (0, t, 0)), _const_spec(perm.shape),
                  _const_spec(wb.shape), _const_spec(a_re.shape), _const_spec(a_im.shape),
                  _const_spec(wc.shape), _const_spec(d.shape), _const_spec(wglu.shape),
                  _const_spec(bglu.shape)],
        out_specs=pl.BlockSpec((nb, tc, width), lambda t: (0, t, 0)),
        out_shape=jax.ShapeDtypeStruct((nb, lp, width), BF16),
        scratch_shapes=[pltpu.VMEM((2 * sc // LANES, rows, LANES), F32),
                        pltpu.VMEM((nb, sc), F32), pltpu.VMEM((nb, sc), F32),
                        pltpu.VMEM((width // LANES, rows, LANES), F32),
                        pltpu.VMEM((rows, width), F32)],
        compiler_params=pltpu.CompilerParams(
            dimension_semantics=("arbitrary",), vmem_limit_bytes=VMEM_LIMIT),
        name="s5_mixer",
    )(u3, perm, wb, a_re, a_im, wc, d, wglu, bglu)


MASKED_LOG = -1e30
SB_UNROLL = 4
SB_LEAD = 3
SB_TRAIL = SB_LEAD + SB_UNROLL - 1


def _sb_kernel(tbl_ref, q_ref, k_ref, v_ref, o_ref, q_scr, km_ref, vm_ref, carry_scr, acc_scr,
               tri_scr, *bufs, tq, kb, n_diag, n_off):
    lp = k_ref.shape[1]
    n_pairs = k_ref.shape[2] // LANES
    n_q = km_ref.shape[0] // n_pairs
    n_heads = LANES // SB_HEAD_DIM
    reps = kb // LANES

    carry_scr[...] = jnp.zeros_like(carry_scr)
    acc_scr[...] = jnp.zeros_like(acc_scr)
    lane = lax.broadcasted_iota(jnp.int32, (kb, LANES), 1)
    for pair in range(n_pairs):
        cols = slice(pair * LANES, (pair + 1) * LANES)
        q0 = pair * n_q * tq
        q_scr[q0:q0 + lp, :] = q_ref[0, :, cols]
        if n_q * tq > lp:
            q_scr[q0 + lp:q0 + n_q * tq, :] = jnp.zeros((n_q * tq - lp, LANES), q_scr.dtype)
        for m in range(n_q):
            valid = min(kb, lp - m * kb)
            k_blk = k_ref[0, m * kb:m * kb + valid, cols]
            v_blk = v_ref[0, m * kb:m * kb + valid, cols]
            if valid < kb:
                pad = jnp.zeros((kb - valid, LANES), k_blk.dtype)
                k_blk = jnp.concatenate([k_blk, pad], axis=0)
                v_blk = jnp.concatenate([v_blk, pad], axis=0)
            for h in range(n_heads):
                in_head = (lane >= h * SB_HEAD_DIM) & (lane < (h + 1) * SB_HEAD_DIM)
                km_ref[pair * n_q + m, h] = jnp.where(in_head, k_blk, jnp.zeros_like(k_blk))
                vm_ref[pair * n_q + m, h] = jnp.where(in_head, v_blk, jnp.zeros_like(v_blk))

    row = lax.broadcasted_iota(jnp.int32, (kb, kb), 0)
    col = lax.broadcasted_iota(jnp.int32, (kb, kb), 1)
    tri = jnp.where(row > col, 1.0, 0.0).astype(BF16)
    below_diag = lax.broadcasted_iota(jnp.int32, (tq, kb), 1) < lax.broadcasted_iota(
        jnp.int32, (tq, kb), 0)

    def item(j):
        return tbl_ref[3 * j], tbl_ref[3 * j + 1], tbl_ref[3 * j + 2]

    z_buf, hl_buf, base_buf, rs_buf, w_buf = (bufs[0:2], bufs[2:4], bufs[4:6], bufs[6:8],
                                              bufs[8:10])
    tri_scr[...] = tri
    for ref in (z_buf[0], hl_buf[1], base_buf[1], rs_buf[1], w_buf[0]):
        ref[...] = jnp.zeros_like(ref)

    def scores(j, z_ref):
        qt, m, _ = item(j)
        q = q_scr[pl.ds(pl.multiple_of(qt * tq, tq), tq), :]
        for h in range(n_heads):
            z_ref[h] = _nt_dot(q, km_ref[m, h])

    def log_terms(h, z_ref, hl_ref, base_ref, rs_ref, diagonal):
        z = z_ref[h]
        nlk = jnp.maximum(z, 0.0) + jnp.log(1.0 + jnp.exp(-jnp.abs(z)))
        base = z - nlk
        if diagonal:
            nlk = jnp.where(below_diag, nlk, 0.0)
            base = jnp.where(below_diag, base, MASKED_LOG)
        hl_ref[h] = nlk.astype(BF16)
        base_ref[h] = base
        rs_ref[h] = jnp.broadcast_to(jnp.sum(nlk, axis=1, keepdims=True), (tq, LANES))

    def suffix_sum(h, j, hl_ref):
        _, _, slot = item(j)
        return (jnp.dot(hl_ref[h], tri_scr[...], preferred_element_type=F32)
                + jnp.concatenate([carry_scr[slot, h]] * reps, axis=1))

    def weights(h, j, later, base_ref, rs_ref, w_ref):
        _, _, slot = item(j)
        w_ref[h] = jnp.exp(base_ref[h] - later).astype(BF16)
        carry_scr[slot, h] = carry_scr[slot, h] + rs_ref[h]

    def weighted_values(j, w_ref):
        _, m, slot = item(j)
        pv = jnp.dot(w_ref[0], vm_ref[m, 0], preferred_element_type=F32)
        for h in range(1, n_heads):
            pv = pv + jnp.dot(w_ref[h], vm_ref[m, h], preferred_element_type=F32)
        acc_scr[slot] = acc_scr[slot] + pv

    def run(first, n_real, diagonal):
        n_iter = -(-(n_real + SB_LEAD) // SB_UNROLL) * SB_UNROLL

        def step(i, p):
            j = first + i
            later = []
            for h in range(n_heads):
                log_terms(h, z_buf[p], hl_buf[p], base_buf[p], rs_buf[p], diagonal)
                if h == 0:
                    weighted_values(j - 2, w_buf[p])
                later.append(suffix_sum(h, j - 1, hl_buf[1 - p]))
            for h in range(n_heads):
                weights(h, j - 1, later[h], base_buf[1 - p], rs_buf[1 - p], w_buf[1 - p])
                if h == 0:
                    scores(j + 1, z_buf[1 - p])

        def body(it, c):
            for k in range(SB_UNROLL):
                step(SB_UNROLL * it + k, k % 2)
            return c

        lax.fori_loop(0, n_iter // SB_UNROLL, body, 0)

    run(SB_LEAD - 1, n_diag, True)
    run(n_diag + SB_LEAD + SB_TRAIL + SB_LEAD - 1, n_off, False)
    for pair in range(n_pairs):
        o_ref[0, :, pair * LANES:(pair + 1) * LANES] = acc_scr[
            pair * n_q:(pair + 1) * n_q].reshape(n_q * tq, LANES)[:lp].astype(o_ref.dtype)


def _sb_items(n_q, n_pairs):
    dummy = [(0, 0, n_pairs * n_q)]
    base = [p * n_q for p in range(n_pairs)]
    diag = [(o + t, o + t, o + t) for o in base for t in range(n_q)]
    off = [(o + t, o + m, o + t) for o in base for t in range(n_q) for m in range(t - 1, -1, -1)]
    flat = (SB_LEAD * dummy + diag + SB_TRAIL * dummy + SB_LEAD * dummy + off + SB_TRAIL * dummy)
    return len(diag), len(off), jnp.asarray([x for e in flat for x in e], jnp.int32)


SB_PAIRS_PER_STEP = 4


def _stick_breaking(qkv3, sbw):
    nb, lp, _ = qkv3.shape
    tq = kb = SB_TILE
    n_heads = LANES // SB_HEAD_DIM
    pps = math.gcd(SB_PAIRS_PER_STEP, sbw // LANES)
    n_steps = sbw // (pps * LANES)
    bw = pps * LANES
    n_q = -(-lp // tq)
    n_diag, n_off, table = _sb_items(n_q, pps)
    masked_kv = pltpu.VMEM((pps * n_q, n_heads, kb, LANES), BF16)
    grid_spec = pltpu.PrefetchScalarGridSpec(
        num_scalar_prefetch=1,
        grid=(nb, n_steps),
        in_specs=[pl.BlockSpec((1, lp, bw), lambda b, p, t: (b, 0, p)),
                  pl.BlockSpec((1, lp, bw), lambda b, p, t: (b, 0, n_steps + p)),
                  pl.BlockSpec((1, lp, bw), lambda b, p, t: (b, 0, 2 * n_steps + p))],
        out_specs=pl.BlockSpec((1, lp, bw), lambda b, p, t: (b, 0, p)),
        scratch_shapes=[pltpu.VMEM((pps * n_q * tq, LANES), BF16), masked_kv, masked_kv,
                        pltpu.VMEM((pps * n_q + 1, n_heads, tq, LANES), F32),
                        pltpu.VMEM((pps * n_q + 1, tq, LANES), F32),
                        pltpu.VMEM((kb, kb), BF16)]
        + 2 * [pltpu.VMEM((n_heads, tq, kb), F32)]
        + 2 * [pltpu.VMEM((n_heads, tq, kb), BF16)]
        + 2 * [pltpu.VMEM((n_heads, tq, kb), F32)]
        + 2 * [pltpu.VMEM((n_heads, tq, LANES), F32)]
        + 2 * [pltpu.VMEM((n_heads, tq, kb), BF16)])
    return pl.pallas_call(
        functools.partial(_sb_kernel, tq=tq, kb=kb, n_diag=n_diag, n_off=n_off),
        grid_spec=grid_spec,
        out_shape=jax.ShapeDtypeStruct((nb, lp, sbw), BF16),
        compiler_params=pltpu.CompilerParams(
            dimension_semantics=("parallel", "parallel"), vmem_limit_bytes=VMEM_LIMIT),
        name="stick_breaking",
    )(table, qkv3, qkv3, qkv3)


def _hgrn_kernel(q_ref, f_ref, v_ref, g_ref, lb_ref, ng_ref, o_ref, *, chunk, group):
    lp = q_ref.shape[1]
    rows = chunk * group
    heads = q_ref.shape[2] // HG_DK
    assert heads == 2
    row = lax.broadcasted_iota(jnp.int32, (chunk, chunk), 0)
    col = lax.broadcasted_iota(jnp.int32, (chunk, chunk), 1)
    causal = col <= row
    tri = jnp.where(causal, 1.0, 0.0).astype(BF16)
    tri2 = jnp.concatenate([tri, tri], axis=1)
    col2 = lax.broadcasted_iota(jnp.int32, (chunk, heads * chunk), 1)
    row2 = lax.broadcasted_iota(jnp.int32, (chunk, heads * chunk), 0)
    causal2 = jnp.where(col2 >= chunk, col2 - chunk, col2) <= row2
    first = lax.broadcasted_iota(jnp.int32, (chunk, heads * HG_DK), 1) < HG_DK
    zero_state = jnp.zeros((HG_DK, HG_DK), BF16)
    lb = lb_ref[...]
    hs = [slice(h * HG_DK, (h + 1) * HG_DK) for h in range(heads)]

    def block_diag(x):
        zero = jnp.zeros_like(x)
        return jnp.concatenate([jnp.where(first, x, zero), jnp.where(first, zero, x)], axis=0)

    def body(s, states):
        sl = pl.ds(pl.multiple_of(s * rows, rows), rows)
        f = lb + (1.0 - lb) * _sigmoid(f_ref[0, sl, :])
        kk = 1.0 - f
        hi, lo = _split_bf16(jnp.log(f))
        cs = [slice(c * chunk, (c + 1) * chunk) for c in range(group)]
        bcum = jnp.concatenate(
            [jnp.dot(tri2, jnp.concatenate([hi[r], lo[r]], axis=0), preferred_element_type=F32)
             for r in cs], axis=0)
        q_dec = (q_ref[0, sl, :] * jnp.exp(bcum)).astype(BF16)
        k_inv_f = kk * jnp.exp(-bcum)
        k_inv = k_inv_f.astype(BF16)
        v = v_ref[0, sl, :].astype(BF16)
        states = list(states)
        outs = []
        for r in cs:
            e_last = jnp.exp(bcum[r.stop - 1:r.stop, :])
            k_end = (k_inv_f[r] * e_last).astype(BF16)
            scores = jnp.where(causal2, _nt_dot(q_dec[r], block_diag(k_inv[r])), 0.0)
            st_bd = jnp.concatenate(
                [jnp.concatenate([states[0].astype(BF16), zero_state], axis=1),
                 jnp.concatenate([zero_state, states[1].astype(BF16)], axis=1)], axis=0)
            outs.append(jnp.dot(scores.astype(BF16), block_diag(v[r]), preferred_element_type=F32)
                        + _nt_dot(q_dec[r], st_bd))
            for h in range(heads):
                states[h] = states[h] * e_last[:, hs[h]] + lax.dot_general(
                    v[r][:, hs[h]], k_end[:, hs[h]], (((0,), (0,)), ((), ())),
                    preferred_element_type=F32)
        o = jnp.concatenate(outs, axis=0)
        o = jnp.concatenate(
            [o[:, c] * lax.rsqrt(jnp.mean(o[:, c] * o[:, c], axis=-1, keepdims=True) + RMS_EPS)
             for c in hs], axis=1)
        g = g_ref[0, sl, :]
        o_ref[0, sl, :] = (o * ng_ref[...] * (g * _sigmoid(g))).astype(o_ref.dtype)
        return tuple(states)

    init = tuple(jnp.zeros((HG_DK, HG_DK), F32) for _ in range(heads))
    lax.fori_loop(0, lp // rows, body, init)


HG_MAX_GROUP = 17


def _hgrn2(proj3, lb, norm_g, width):
    nb, lp, _ = proj3.shape
    bw = 2 * HG_DK
    n_blk = width // bw
    n_chunks = lp // HG_CHUNK
    group = max(g for g in range(1, HG_MAX_GROUP + 1) if n_chunks % g == 0)

    def col_spec(part):
        return pl.BlockSpec((1, lp, bw), lambda b, j, part=part: (b, 0, part * n_blk + j))

    vec_spec = pl.BlockSpec((1, bw), lambda b, j: (0, j))
    return pl.pallas_call(
        functools.partial(_hgrn_kernel, chunk=HG_CHUNK, group=group),
        grid=(nb, n_blk),
        in_specs=[col_spec(0), col_spec(1), col_spec(2), col_spec(3), vec_spec, vec_spec],
        out_specs=pl.BlockSpec((1, lp, bw), lambda b, j: (b, 0, j)),
        out_shape=jax.ShapeDtypeStruct((nb, lp, width), BF16),
        compiler_params=pltpu.CompilerParams(
            dimension_semantics=("parallel", "parallel"), vmem_limit_bytes=VMEM_LIMIT),
        name="hgrn2_mixer",
    )(proj3, proj3, proj3, proj3, lb, norm_g)


def _block_tail_kernel(*refs, n_mix, alpha, ff_chunks):
    h_ref = refs[0]
    mix_refs = refs[1:1 + n_mix]
    w_refs = refs[1 + n_mix:1 + 2 * n_mix]
    (g1_ref, b1_ref, wup_ref, bup_ref, wdn_ref, bdn_ref, g2_ref, b2_ref, o_ref) = refs[1 + 2 * n_mix:]
    mix = jnp.dot(mix_refs[0][...], w_refs[0][...], preferred_element_type=F32)
    for m_ref, w_ref in zip(mix_refs[1:], w_refs[1:]):
        mix = mix + jnp.dot(m_ref[...], w_ref[...], preferred_element_type=F32)
    h1 = _layer_norm(alpha * h_ref[...] + mix, g1_ref[...], b1_ref[...])
    h1b = h1.astype(BF16)
    dff = wup_ref.shape[1]
    fc = dff // ff_chunks
    acc = alpha * h1 + bdn_ref[...]
    for c in range(ff_chunks):
        hid = jnp.dot(h1b, wup_ref[:, c * fc:(c + 1) * fc], preferred_element_type=F32)
        hid = jnp.maximum(hid + bup_ref[:, c * fc:(c + 1) * fc], 0.0)
        hid = (hid * hid).astype(BF16)
        acc = acc + jnp.dot(hid, wdn_ref[c * fc:(c + 1) * fc, :], preferred_element_type=F32)
    o_ref[...] = _layer_norm(acc, g2_ref[...], b2_ref[...])


def _block_tail(h, mixes, ws, g1, b1, wup, bup, wdn, bdn, g2, b2, alpha, window=None):
    rows, d = h.shape
    consts = [g1, b1, wup, bup, wdn, bdn, g2, b2]
    if window is None:
        tm = _row_tile(rows, 512)
        grid = (rows // tm,)
        row_spec = lambda n: pl.BlockSpec((tm, n), lambda i: (i, 0))
        out_spec, out_rows = row_spec(d), rows
    else:
        nb, lp, start, count = window
        tm = _row_tile(count, 512)
        per_seq = count // tm
        grid = (nb, per_seq)
        align = math.gcd(lp, start, tm)
        row_spec = lambda n: pl.BlockSpec(
            (pl.Element(tm), pl.Element(n)),
            lambda b, j: (pl.multiple_of(b * lp + start + j * tm, align), 0))
        out_spec = pl.BlockSpec((tm, d), lambda b, j: (b * per_seq + j, 0))
        out_rows = nb * count
    return pl.pallas_call(
        functools.partial(_block_tail_kernel, n_mix=len(mixes), alpha=alpha, ff_chunks=4),
        grid=grid,
        in_specs=([row_spec(d)] + [row_spec(m.shape[1]) for m in mixes]
                  + [_const_spec(w.shape) for w in ws] + [_const_spec(c.shape) for c in consts]),
        out_specs=out_spec,
        out_shape=jax.ShapeDtypeStruct((out_rows, d), F32),
        compiler_params=pltpu.CompilerParams(
            dimension_semantics=("parallel",) * len(grid), vmem_limit_bytes=VMEM_LIMIT),
        name="block_tail",
    )(h, *mixes, *ws, *consts)


def kernel(x, meta, w_in_ab, s5_lam_re, s5_lam_im, s5_log_dt, s5_b_re, s5_b_im, s5_c_re, s5_c_im, s5_d, s5_w_glu, s5_b_glu, w_out_ab, w_in_c, hgrn_gamma, hgrn_norm_g, w_out_c, ln_mix_g, ln_mix_b, mlp_w_up, mlp_b_up, mlp_w_down, mlp_b_down, ln_mlp_g, ln_mlp_b):
    nb, seq, d = x.shape
    depth = ln_mix_g.shape[0]
    assert depth == 2 and w_in_ab.shape[0] == 1 and w_in_c.shape[0] == 1
    alpha = (2.0 * depth) ** 0.25
    s5w = s5_w_glu.shape[1]
    sbw = (w_in_ab.shape[2] - s5w) // 3
    hgw = w_in_c.shape[2] // 4
    ltok = N_META + seq
    lp = -(-ltok // SEQ_ALIGN) * SEQ_ALIGN
    rows = nb * lp
    row2 = lambda a: a.reshape(1, -1).astype(F32)

    h = jnp.concatenate([jnp.broadcast_to(meta.astype(x.dtype)[None], (nb, N_META, d)), x,
                         jnp.zeros((nb, lp - ltok, d), x.dtype)], axis=1).reshape(rows, d)

    u, qkv = _inproj_ab(h, w_in_ab[0].astype(BF16), s5w, sbw)
    wb, wc, a_re, a_im = _s5_params(s5_lam_re[0], s5_lam_im[0], s5_log_dt[0], s5_b_re[0],
                                    s5_b_im[0], s5_c_re[0], s5_c_im[0])
    a_out = _s5(u.reshape(nb, lp, s5w), wb, a_re, a_im, wc, row2(s5_d[0]),
                s5_w_glu[0].astype(BF16), row2(s5_b_glu[0]))
    b_out = _stick_breaking(qkv.reshape(nb, lp, 3 * sbw), sbw)
    w_out = w_out_ab[0].astype(BF16)
    h = _block_tail(h, [a_out.reshape(rows, s5w), b_out.reshape(rows, sbw)],
                    [w_out[:s5w], w_out[s5w:]],
                    row2(ln_mix_g[0]), row2(ln_mix_b[0]), mlp_w_up[0].astype(BF16),
                    row2(mlp_b_up[0]), mlp_w_down[0].astype(BF16), row2(mlp_b_down[0]),
                    row2(ln_mlp_g[0]), row2(ln_mlp_b[0]), alpha)

    proj = _inproj_c(h, w_in_c[0].astype(BF16))
    p = jax.nn.softmax(hgrn_gamma.astype(F32), axis=0)
    lb = (jnp.cumsum(p, axis=0) - p[0])[1]
    c_out = _hgrn2(proj.reshape(nb, lp, 4 * hgw), row2(lb), row2(hgrn_norm_g[0]), hgw)
    h = _block_tail(h, [c_out.reshape(rows, hgw)], [w_out_c[0].astype(BF16)],
                    row2(ln_mix_g[1]), row2(ln_mix_b[1]), mlp_w_up[1].astype(BF16),
                    row2(mlp_b_up[1]), mlp_w_down[1].astype(BF16), row2(mlp_b_down[1]),
                    row2(ln_mlp_g[1]), row2(ln_mlp_b[1]), alpha, window=(nb, lp, N_META, seq))
    return h.reshape(nb, seq, d)
```

```python
import functools
import math

import jax
import jax.numpy as jnp
from jax import lax
from jax.experimental import pallas as pl
from jax.experimental.pallas import tpu as pltpu

F32 = jnp.float32
BF16 = jnp.bfloat16

N_META = 16
S5_GROUP = 16
S5_STATE = 64
SB_HEAD_DIM = 64
HG_DK = 128
HG_CHUNK = 64
LN_EPS = 1e-5
RMS_EPS = 1e-6

LANES = 128
SEQ_ALIGN = 64
SB_TILE = 256
VMEM_LIMIT = 56 * 1024 * 1024


def _row_tile(rows, target):
    best = 8
    for t in range(8, min(rows, target) + 1, 8):
        if rows % t == 0:
            best = t
    return best


def _const_spec(shape):
    nd = len(shape)
    return pl.BlockSpec(shape, lambda *_: (0,) * nd, pipeline_mode=pl.Buffered(1))


def _nt_dot(a, b):
    return lax.dot_general(a, b, (((1,), (1,)), ((), ())), preferred_element_type=F32)


def _split_bf16(x):
    hi = x.astype(BF16)
    lo = (x - hi.astype(F32)).astype(BF16)
    return hi, lo


def _layer_norm(x, g, b):
    mu = jnp.mean(x, axis=-1, keepdims=True)
    xc = x - mu
    var = jnp.mean(xc * xc, axis=-1, keepdims=True)
    return xc * lax.rsqrt(var + LN_EPS) * g + b


def _sigmoid(x):
    return 1.0 / (1.0 + jnp.exp(-x))


def _inproj_ab_kernel(h_ref, w_ref, u_ref, qkv_ref, *, s5w, sbw):
    h = h_ref[...].astype(BF16)
    u_ref[...] = jnp.dot(h, w_ref[:, :s5w], preferred_element_type=F32)
    q = jnp.dot(h, w_ref[:, s5w:s5w + sbw], preferred_element_type=F32)
    qkv_ref[:, :sbw] = (q * (SB_HEAD_DIM ** -0.5)).astype(BF16)
    kv = jnp.dot(h, w_ref[:, s5w + sbw:], preferred_element_type=F32)
    qkv_ref[:, sbw:] = kv.astype(BF16)


def _inproj_ab(h, w, s5w, sbw):
    rows, d = h.shape
    tm = _row_tile(rows, 512)
    return pl.pallas_call(
        functools.partial(_inproj_ab_kernel, s5w=s5w, sbw=sbw),
        grid=(rows // tm,),
        in_specs=[pl.BlockSpec((tm, d), lambda i: (i, 0)), _const_spec(w.shape)],
        out_specs=[pl.BlockSpec((tm, s5w), lambda i: (i, 0)),
                   pl.BlockSpec((tm, 3 * sbw), lambda i: (i, 0))],
        out_shape=[jax.ShapeDtypeStruct((rows, s5w), F32),
                   jax.ShapeDtypeStruct((rows, 3 * sbw), BF16)],
        compiler_params=pltpu.CompilerParams(
            dimension_semantics=("parallel",), vmem_limit_bytes=VMEM_LIMIT),
        name="inproj_ab",
    )(h, w)


def _inproj_c_kernel(h_ref, w_ref, o_ref, *, n_split):
    h = h_ref[...].astype(BF16)
    n = w_ref.shape[1] // n_split
    for s in range(n_split):
        o_ref[:, s * n:(s + 1) * n] = jnp.dot(
            h, w_ref[:, s * n:(s + 1) * n], preferred_element_type=F32)


def _inproj_c(h, w):
    rows, d = h.shape
    n = w.shape[1]
    tm = _row_tile(rows, 512)
    return pl.pallas_call(
        functools.partial(_inproj_c_kernel, n_split=4),
        grid=(rows // tm,),
        in_specs=[pl.BlockSpec((tm, d), lambda i: (i, 0)), _const_spec(w.shape)],
        out_specs=pl.BlockSpec((tm, n), lambda i: (i, 0)),
        out_shape=jax.ShapeDtypeStruct((rows, n), F32),
        compiler_params=pltpu.CompilerParams(
            dimension_semantics=("parallel",), vmem_limit_bytes=VMEM_LIMIT),
        name="inproj_c",
    )(h, w)


S5_SCAN_UNROLL = 2


def _s5_kernel(u_ref, perm_ref, wb_ref, are_ref, aim_ref, wc_ref, d_ref, wglu_ref, bglu_ref,
               o_ref, bu_ref, sre_ref, sim_ref, ytm_ref, y_ref, *, nb, tc, slab_group):
    width = u_ref.shape[2]
    sc = are_ref.shape[1]
    nq = width // LANES
    per_q = sc // nq

    @pl.when(pl.program_id(0) == 0)
    def _():
        sre_ref[...] = jnp.zeros_like(sre_ref)
        sim_ref[...] = jnp.zeros_like(sim_ref)

    n_slabs = sc // LANES
    slabs_q = per_q // LANES
    u = u_ref[...].reshape(nb * tc, width)
    u_tm = jnp.dot(perm_ref[...], u.astype(BF16), preferred_element_type=F32).astype(BF16)
    for q in range(nq):
        r = jnp.dot(u_tm[:, q * LANES:(q + 1) * LANES], wb_ref[q], preferred_element_type=F32)
        for j in range(slabs_q):
            bu_ref[q * slabs_q + j] = r[:, j * LANES:(j + 1) * LANES]
            bu_ref[n_slabs + q * slabs_q + j] = r[:, per_q + j * LANES:per_q + (j + 1) * LANES]

    for s0 in range(0, n_slabs, slab_group):
        slabs = list(range(s0, s0 + slab_group))
        a_re = [jnp.broadcast_to(are_ref[:, s * LANES:(s + 1) * LANES], (nb, LANES)) for s in slabs]
        a_im = [jnp.broadcast_to(aim_ref[:, s * LANES:(s + 1) * LANES], (nb, LANES)) for s in slabs]
        init = []
        for s in slabs:
            init += [sre_ref[:, s * LANES:(s + 1) * LANES], sim_ref[:, s * LANES:(s + 1) * LANES]]

        def body(t, carry, slabs=slabs, a_re=a_re, a_im=a_im):
            new = []
            rows = pl.ds(pl.multiple_of(t * nb, nb), nb)
            for i, s in enumerate(slabs):
                sr, si = carry[2 * i], carry[2 * i + 1]
                nr = a_re[i] * sr - a_im[i] * si + bu_ref[s, rows, :]
                ni = a_re[i] * si + a_im[i] * sr + bu_ref[n_slabs + s, rows, :]
                bu_ref[s, rows, :] = nr
                bu_ref[n_slabs + s, rows, :] = ni
                new += [nr, ni]
            return tuple(new)

        fin = lax.fori_loop(0, tc, body, tuple(init), unroll=S5_SCAN_UNROLL)
        for i, s in enumerate(slabs):
            sre_ref[:, s * LANES:(s + 1) * LANES] = fin[2 * i]
            sim_ref[:, s * LANES:(s + 1) * LANES] = fin[2 * i + 1]

    for q in range(nq):
        s_re = jnp.concatenate(
            [bu_ref[q * slabs_q + j].astype(BF16) for j in range(slabs_q)], axis=1)
        s_im = jnp.concatenate(
            [bu_ref[n_slabs + q * slabs_q + j].astype(BF16) for j in range(slabs_q)], axis=1)
        ytm_ref[q] = (jnp.dot(s_re, wc_ref[q, :per_q, :], preferred_element_type=F32)
                      + jnp.dot(s_im, wc_ref[q, per_q:, :], preferred_element_type=F32))
    for q in range(nq):
        for b in range(nb):
            y_ref[b * tc:(b + 1) * tc, q * LANES:(q + 1) * LANES] = (
                ytm_ref[q, pl.ds(b, tc, stride=nb), :])
    y = y_ref[...] + d_ref[...] * u
    y = 0.5 * y * (1.0 + jnp.tanh(math.sqrt(2.0 / math.pi) * (y + 0.044715 * (y * y * y))))
    gate =jnp.dot(y.astype(BF16), wglu_ref[...], preferred_element_type=F32) + bglu_ref[...]
    o_ref[...] = (y * _sigmoid(gate)).reshape(nb, tc, width).astype(o_ref.dtype)


def _s5_params(lam_re, lam_im, log_dt, b_re, b_im, c_re, c_im):
    g, p = lam_re.shape
    lr, li = lam_re.astype(F32), lam_im.astype(F32)
    dt = jnp.exp(log_dt.astype(F32))[:, None]
    mag = jnp.exp(lr * dt)
    a_re = mag * jnp.cos(li * dt)
    a_im = mag * jnp.sin(li * dt)
    den = lr * lr + li * li
    c_re_ = ((a_re - 1.0) * lr + a_im * li) / den
    c_im_ = (a_im * lr - (a_re - 1.0) * li) / den
    bb_re = c_re_[:, :, None] * b_re.astype(F32) - c_im_[:, :, None] * b_im.astype(F32)
    bb_im = c_re_[:, :, None] * b_im.astype(F32) + c_im_[:, :, None] * b_re.astype(F32)
    gq = LANES // S5_GROUP
    nq = g // gq
    eye = jnp.eye(gq, dtype=F32)

    def b_slab(x):
        x = x.reshape(nq, gq, p, S5_GROUP)
        return jnp.einsum('qgph,gk->qghkp', x, eye).reshape(nq, gq * S5_GROUP, gq * p)

    def c_slab(x):
        x = x.reshape(nq, gq, S5_GROUP, p)
        return jnp.einsum('qghp,gk->qgpkh', x, eye).reshape(nq, gq * p, gq * S5_GROUP)

    wb = jnp.concatenate([b_slab(bb_re), b_slab(bb_im)], axis=2)
    wc = jnp.concatenate([c_slab(c_re.astype(F32)), c_slab(-c_im.astype(F32))], axis=1)
    return wb.astype(BF16), wc.astype(BF16), a_re.reshape(1, g * p), a_im.reshape(1, g * p)


def _s5(u3, wb, a_re, a_im, wc, d, wglu, bglu):
    nb, lp, width = u3.shape
    sc = a_re.shape[1]
    tc = 64
    assert lp % tc == 0
    rows = nb * tc
    r = jnp.arange(rows)
    perm = (jnp.arange(rows)[None, :] == ((r % nb) * tc + r // nb)[:, None]).astype(BF16)
    return pl.pallas_call(
        functools.partial(_s5_kernel, nb=nb, tc=tc, slab_group=4),
        grid=(lp // tc,),
        in_specs=[pl.BlockSpec((nb, tc, width), lambda t: (0, t, 0)), _const_spec(perm.shape),
                  _const_spec(wb.shape), _const_spec(a_re.shape), _const_spec(a_im.shape),
                  _const_spec(wc.shape), _const_spec(d.shape), _const_spec(wglu.shape),
                  _const_spec(bglu.shape)],
        out_specs=pl.BlockSpec((nb, tc, width), lambda t: (0, t, 0)),
        out_shape=jax.ShapeDtypeStruct((nb, lp, width), BF16),
        scratch_shapes=[pltpu.VMEM((2 * sc // LANES, rows, LANES), F32),
                        pltpu.VMEM((nb, sc), F32), pltpu.VMEM((nb, sc), F32),
                        pltpu.VMEM((width // LANES, rows, LANES), F32),
                        pltpu.VMEM((rows, width), F32)],
        compiler_params=pltpu.CompilerParams(
            dimension_semantics=("arbitrary",), vmem_limit_bytes=VMEM_LIMIT),
        name="s5_mixer",
    )(u3, perm, wb, a_re, a_im, wc, d, wglu, bglu)


MASKED_LOG = -1e30
SB_UNROLL = 4
SB_LEAD = 3
SB_TRAIL = SB_LEAD + SB_UNROLL - 1


def _sb_kernel(tbl_ref, q_ref, k_ref, v_ref, o_ref, q_scr, km_ref, vm_ref, carry_scr, acc_scr,
               tri_scr, *bufs, tq, kb, n_diag, n_off, n_tail):
    lp = k_ref.shape[1]
    n_pairs = k_ref.shape[2] // LANES
    n_q = km_ref.shape[0] // n_pairs
    n_heads = LANES // SB_HEAD_DIM
    reps = kb // LANES

    carry_scr[...] = jnp.zeros_like(carry_scr)
    acc_scr[...] = jnp.zeros_like(acc_scr)
    lane = lax.broadcasted_iota(jnp.int32, (kb, LANES), 1)
    for pair in range(n_pairs):
        cols = slice(pair * LANES, (pair + 1) * LANES)
        q0 = pair * n_q * tq
        q_scr[q0:q0 + lp, :] = q_ref[0, :, cols]
        if n_q * tq > lp:
            q_scr[q0 + lp:q0 + n_q * tq, :] = jnp.zeros((n_q * tq - lp, LANES), q_scr.dtype)
        for m in range(n_q):
            valid = min(kb, lp - m * kb)
            k_blk = k_ref[0, m * kb:m * kb + valid, cols]
            v_blk = v_ref[0, m * kb:m * kb + valid, cols]
            if valid < kb:
                pad = jnp.zeros((kb - valid, LANES), k_blk.dtype)
                k_blk = jnp.concatenate([k_blk, pad], axis=0)
                v_blk = jnp.concatenate([v_blk, pad], axis=0)
            for h in range(n_heads):
                in_head = (lane >= h * SB_HEAD_DIM) & (lane < (h + 1) * SB_HEAD_DIM)
                km_ref[pair * n_q + m, h] = jnp.where(in_head, k_blk, jnp.zeros_like(k_blk))
                vm_ref[pair * n_q + m, h] = jnp.where(in_head, v_blk, jnp.zeros_like(v_blk))

    row = lax.broadcasted_iota(jnp.int32, (kb, kb), 0)
    col = lax.broadcasted_iota(jnp.int32, (kb, kb), 1)
    tri = jnp.where(row > col, 1.0, 0.0).astype(BF16)
    below_diag = lax.broadcasted_iota(jnp.int32, (tq, kb), 1) < lax.broadcasted_iota(
        jnp.int32, (tq, kb), 0)

    def item(j):
        return tbl_ref[3 * j], tbl_ref[3 * j + 1], tbl_ref[3 * j + 2]

    z_buf, hl_buf, base_buf, rs_buf, w_buf = (bufs[0:2], bufs[2:4], bufs[4:6], bufs[6:8],
                                              bufs[8:10])
    tri_scr[...] = tri
    for ref in (z_buf[0], hl_buf[1], base_buf[1], rs_buf[1], w_buf[0]):
        ref[...] = jnp.zeros_like(ref)

    def scores(j, z_ref, tm):
        qt, m, _ = item(j)
        q = q_scr[pl.ds(pl.multiple_of(qt * tq, tq), tm), :]
        for h in range(n_heads):
            z_ref[h, :tm] = _nt_dot(q, km_ref[m, h])

    def log_terms(h, z_ref, hl_ref, base_ref, rs_ref, diagonal, tm):
        z = z_ref[h, :tm]
        nlk = jnp.maximum(z, 0.0) + jnp.log(1.0 + jnp.exp(-jnp.abs(z)))
        base = z - nlk
        if diagonal:
            nlk = jnp.where(below_diag, nlk, 0.0)
            base = jnp.where(below_diag, base, MASKED_LOG)
        hl_ref[h, :tm] = nlk.astype(BF16)
        base_ref[h, :tm] = base
        rs_ref[h, :tm] = jnp.broadcast_to(jnp.sum(nlk, axis=1, keepdims=True), (tm, LANES))

    def suffix_sum(h, j, hl_ref, tm):
        _, _, slot = item(j)
        return (jnp.dot(hl_ref[h, :tm], tri_scr[...], preferred_element_type=F32)
                + jnp.concatenate([carry_scr[slot, h, :tm]] * reps, axis=1))

    def weights(h, j, later, base_ref, rs_ref, w_ref, tm):
        _, _, slot = item(j)
        w_ref[h, :tm] = jnp.exp(base_ref[h, :tm] - later).astype(BF16)
        carry_scr[slot, h, :tm] = carry_scr[slot, h, :tm] + rs_ref[h, :tm]

    def weighted_values(j, w_ref, tm):
        _, m, slot = item(j)
        pv = jnp.dot(w_ref[0, :tm], vm_ref[m, 0], preferred_element_type=F32)
        for h in range(1, n_heads):
            pv = pv + jnp.dot(w_ref[h, :tm], vm_ref[m, h], preferred_element_type=F32)
        acc_scr[slot, :tm] = acc_scr[slot, :tm] + pv

    def run(first, n_real, diagonal, tm):
        n_iter = -(-(n_real + SB_LEAD) // SB_UNROLL) * SB_UNROLL

        def step(i, p):
            j = first + i
            later = []
            for h in range(n_heads):
                log_terms(h, z_buf[p], hl_buf[p], base_buf[p], rs_buf[p], diagonal, tm)
                if h == 0:
                    weighted_values(j - 2, w_buf[p], tm)
                later.append(suffix_sum(h, j - 1, hl_buf[1 - p], tm))
            for h in range(n_heads):
                weights(h, j - 1, later[h], base_buf[1 - p], rs_buf[1 - p], w_buf[1 - p], tm)
                if h == 0:
                    scores(j + 1, z_buf[1 - p], tm)

        def body(it, c):
            for k in range(SB_UNROLL):
                step(SB_UNROLL * it + k, k % 2)
            return c

        lax.fori_loop(0, n_iter // SB_UNROLL, body, 0)

    tail = lp - (n_q - 1) * tq
    start = 0
    for n_items, diagonal, tm in ((n_diag, True, tq), (n_off, False, tq), (n_tail, False, tail)):
        if n_items:
            run(start + SB_LEAD - 1, n_items, diagonal, tm)
        start += SB_LEAD + n_items + SB_TRAIL
    for pair in range(n_pairs):
        o_ref[0, :, pair * LANES:(pair + 1) * LANES] = acc_scr[
            pair * n_q:(pair + 1) * n_q].reshape(n_q * tq, LANES)[:lp].astype(o_ref.dtype)


def _sb_items(n_q, n_pairs, partial_last):
    dummy = [(0, 0, n_pairs * n_q)]
    base = [p * n_q for p in range(n_pairs)]
    n_full = n_q - 1 if partial_last else n_q
    diag = [(o + t, o + t, o + t) for o in base for t in range(n_q)]
    off = [(o + t, o + m, o + t) for o in base for t in range(n_full) for m in range(t - 1, -1, -1)]
    tail = [(o + t, o + m, o + t) for o in base for t in range(n_full, n_q)
            for m in range(t - 1, -1, -1)]
    flat = []
    for items in (diag, off, tail):
        flat += SB_LEAD * dummy + items + SB_TRAIL * dummy
    counts = (len(diag), len(off), len(tail))
    return counts, jnp.asarray([x for e in flat for x in e], jnp.int32)


SB_PAIRS_PER_STEP = 4


def _stick_breaking(qkv3, sbw):
    nb, lp, _ = qkv3.shape
    tq = kb = SB_TILE
    n_heads = LANES // SB_HEAD_DIM
    pps = math.gcd(SB_PAIRS_PER_STEP, sbw // LANES)
    n_steps = sbw // (pps * LANES)
    bw = pps * LANES
    n_q = -(-lp // tq)
    (n_diag, n_off, n_tail), table = _sb_items(n_q, pps, partial_last=lp % tq != 0)
    masked_kv = pltpu.VMEM((pps * n_q, n_heads, kb, LANES), BF16)
    grid_spec = pltpu.PrefetchScalarGridSpec(
        num_scalar_prefetch=1,
        grid=(nb, n_steps),
        in_specs=[pl.BlockSpec((1, lp, bw), lambda b, p, t: (b, 0, p)),
                  pl.BlockSpec((1, lp, bw), lambda b, p, t: (b, 0, n_steps + p)),
                  pl.BlockSpec((1, lp, bw), lambda b, p, t: (b, 0, 2 * n_steps + p))],
        out_specs=pl.BlockSpec((1, lp, bw), lambda b, p, t: (b, 0, p)),
        scratch_shapes=[pltpu.VMEM((pps * n_q * tq, LANES), BF16), masked_kv, masked_kv,
                        pltpu.VMEM((pps * n_q + 1, n_heads, tq, LANES), F32),
                        pltpu.VMEM((pps * n_q + 1, tq, LANES), F32),
                        pltpu.VMEM((kb, kb), BF16)]
        + 2 * [pltpu.VMEM((n_heads, tq, kb), F32)]
        + 2 * [pltpu.VMEM((n_heads, tq, kb), BF16)]
        + 2 * [pltpu.VMEM((n_heads, tq, kb), F32)]
        + 2 * [pltpu.VMEM((n_heads, tq, LANES), F32)]
        + 2 * [pltpu.VMEM((n_heads, tq, kb), BF16)])
    return pl.pallas_call(
        functools.partial(_sb_kernel, tq=tq, kb=kb, n_diag=n_diag, n_off=n_off, n_tail=n_tail),
        grid_spec=grid_spec,
        out_shape=jax.ShapeDtypeStruct((nb, lp, sbw), BF16),
        compiler_params=pltpu.CompilerParams(
            dimension_semantics=("parallel", "parallel"), vmem_limit_bytes=VMEM_LIMIT),
        name="stick_breaking",
    )(table, qkv3, qkv3, qkv3)


def _hgrn_kernel(q_ref, f_ref, v_ref, g_ref, lb_ref, ng_ref, o_ref, *, chunk, group):
    lp = q_ref.shape[1]
    rows = chunk * group
    heads = q_ref.shape[2] // HG_DK
    assert heads == 2
    row = lax.broadcasted_iota(jnp.int32, (chunk, chunk), 0)
    col = lax.broadcasted_iota(jnp.int32, (chunk, chunk), 1)
    causal = col <= row
    tri = jnp.where(causal, 1.0, 0.0).astype(BF16)
    tri2 = jnp.concatenate([tri, tri], axis=1)
    col2 = lax.broadcasted_iota(jnp.int32, (chunk, heads * chunk), 1)
    row2 = lax.broadcasted_iota(jnp.int32, (chunk, heads * chunk), 0)
    causal2 = jnp.where(col2 >= chunk, col2 - chunk, col2) <= row2
    first = lax.broadcasted_iota(jnp.int32, (chunk, heads * HG_DK), 1) < HG_DK
    zero_state = jnp.zeros((HG_DK, HG_DK), BF16)
    lb = lb_ref[...]
    hs = [slice(h * HG_DK, (h + 1) * HG_DK) for h in range(heads)]

    def block_diag(x):
        zero = jnp.zeros_like(x)
        return jnp.concatenate([jnp.where(first, x, zero), jnp.where(first, zero, x)], axis=0)

    def body(s, states):
        sl = pl.ds(pl.multiple_of(s * rows, rows), rows)
        f = lb + (1.0 - lb) * _sigmoid(f_ref[0, sl, :])
        kk = 1.0 - f
        hi, lo = _split_bf16(jnp.log(f))
        cs = [slice(c * chunk, (c + 1) * chunk) for c in range(group)]
        bcum = jnp.concatenate(
            [jnp.dot(tri2, jnp.concatenate([hi[r], lo[r]], axis=0), preferred_element_type=F32)
             for r in cs], axis=0)
        q_dec = (q_ref[0, sl, :] * jnp.exp(bcum)).astype(BF16)
        k_inv_f = kk * jnp.exp(-bcum)
        k_inv = k_inv_f.astype(BF16)
        v = v_ref[0, sl, :].astype(BF16)
        states = list(states)
        outs = []
        for r in cs:
            e_last = jnp.exp(bcum[r.stop - 1:r.stop, :])
            k_end = (k_inv_f[r] * e_last).astype(BF16)
            scores = jnp.where(causal2, _nt_dot(q_dec[r], block_diag(k_inv[r])), 0.0)
            st_bd = jnp.concatenate(
                [jnp.concatenate([states[0].astype(BF16), zero_state], axis=1),
                 jnp.concatenate([zero_state, states[1].astype(BF16)], axis=1)], axis=0)
            outs.append(jnp.dot(scores.astype(BF16), block_diag(v[r]), preferred_element_type=F32)
                        + _nt_dot(q_dec[r], st_bd))
            for h in range(heads):
                states[h] = states[h] * e_last[:, hs[h]] + lax.dot_general(
                    v[r][:, hs[h]], k_end[:, hs[h]], (((0,), (0,)), ((), ())),
                    preferred_element_type=F32)
        o = jnp.concatenate(outs, axis=0)
        o = jnp.concatenate(
            [o[:, c] * lax.rsqrt(jnp.mean(o[:, c] * o[:, c], axis=-1, keepdims=True) + RMS_EPS)
             for c in hs], axis=1)
        g = g_ref[0, sl, :]
        o_ref[0, sl, :] = (o * ng_ref[...] * (g * _sigmoid(g))).astype(o_ref.dtype)
        return tuple(states)

    init = tuple(jnp.zeros((HG_DK, HG_DK), F32) for _ in range(heads))
    lax.fori_loop(0, lp // rows, body, init)


HG_MAX_GROUP = 17


def _hgrn2(proj3, lb, norm_g, width):
    nb, lp, _ = proj3.shape
    bw = 2 * HG_DK
    n_blk = width // bw
    n_chunks = lp // HG_CHUNK
    group = max(g for g in range(1, HG_MAX_GROUP + 1) if n_chunks % g == 0)

    def col_spec(part):
        return pl.BlockSpec((1, lp, bw), lambda b, j, part=part: (b, 0, part * n_blk + j))

    vec_spec = pl.BlockSpec((1, bw), lambda b, j: (0, j))
    return pl.pallas_call(
        functools.partial(_hgrn_kernel, chunk=HG_CHUNK, group=group),
        grid=(nb, n_blk),
        in_specs=[col_spec(0), col_spec(1), col_spec(2), col_spec(3), vec_spec, vec_spec],
        out_specs=pl.BlockSpec((1, lp, bw), lambda b, j: (b, 0, j)),
        out_shape=jax.ShapeDtypeStruct((nb, lp, width), BF16),
        compiler_params=pltpu.CompilerParams(
            dimension_semantics=("parallel", "parallel"), vmem_limit_bytes=VMEM_LIMIT),
        name="hgrn2_mixer",
    )(proj3, proj3, proj3, proj3, lb, norm_g)


def _block_tail_kernel(*refs, n_mix, alpha, ff_chunks):
    h_ref = refs[0]
    mix_refs = refs[1:1 + n_mix]
    w_refs = refs[1 + n_mix:1 + 2 * n_mix]
    (g1_ref, b1_ref, wup_ref, bup_ref, wdn_ref, bdn_ref, g2_ref, b2_ref, o_ref) = refs[1 + 2 * n_mix:]
    mix = jnp.dot(mix_refs[0][...], w_refs[0][...], preferred_element_type=F32)
    for m_ref, w_ref in zip(mix_refs[1:], w_refs[1:]):
        mix = mix + jnp.dot(m_ref[...], w_ref[...], preferred_element_type=F32)
    h1 = _layer_norm(alpha * h_ref[...] + mix, g1_ref[...], b1_ref[...])
    h1b = h1.astype(BF16)
    dff = wup_ref.shape[1]
    fc = dff // ff_chunks
    acc = alpha * h1 + bdn_ref[...]
    for c in range(ff_chunks):
        hid = jnp.dot(h1b, wup_ref[:, c * fc:(c + 1) * fc], preferred_element_type=F32)
        hid = jnp.maximum(hid + bup_ref[:, c * fc:(c + 1) * fc], 0.0)
        hid = (hid * hid).astype(BF16)
        acc = acc + jnp.dot(hid, wdn_ref[c * fc:(c + 1) * fc, :], preferred_element_type=F32)
    o_ref[...] = _layer_norm(acc, g2_ref[...], b2_ref[...])


def _block_tail(h, mixes, ws, g1, b1, wup, bup, wdn, bdn, g2, b2, alpha, window=None):
    rows, d = h.shape
    consts = [g1, b1, wup, bup, wdn, bdn, g2, b2]
    if window is None:
        tm = _row_tile(rows, 512)
        grid = (rows // tm,)
        row_spec = lambda n: pl.BlockSpec((tm, n), lambda i: (i, 0))
        out_spec, out_rows = row_spec(d), rows
    else:
        nb, lp, start, count = window
        tm = _row_tile(count, 512)
        per_seq = count // tm
        grid = (nb, per_seq)
        align = math.gcd(lp, start, tm)
        row_spec = lambda n: pl.BlockSpec(
            (pl.Element(tm), pl.Element(n)),
            lambda b, j: (pl.multiple_of(b * lp + start + j * tm, align), 0))
        out_spec = pl.BlockSpec((tm, d), lambda b, j: (b * per_seq + j, 0))
        out_rows = nb * count
    return pl.pallas_call(
        functools.partial(_block_tail_kernel, n_mix=len(mixes), alpha=alpha, ff_chunks=4),
        grid=grid,
        in_specs=([row_spec(d)] + [row_spec(m.shape[1]) for m in mixes]
                  + [_const_spec(w.shape) for w in ws] + [_const_spec(c.shape) for c in consts]),
        out_specs=out_spec,
        out_shape=jax.ShapeDtypeStruct((out_rows, d), F32),
        compiler_params=pltpu.CompilerParams(
            dimension_semantics=("parallel",) * len(grid), vmem_limit_bytes=VMEM_LIMIT),
        name="block_tail",
    )(h, *mixes, *ws, *consts)


def kernel(x, meta, w_in_ab, s5_lam_re, s5_lam_im, s5_log_dt, s5_b_re, s5_b_im, s5_c_re, s5_c_im, s5_d, s5_w_glu, s5_b_glu, w_out_ab, w_in_c, hgrn_gamma, hgrn_norm_g, w_out_c, ln_mix_g, ln_mix_b, mlp_w_up, mlp_b_up, mlp_w_down, mlp_b_down, ln_mlp_g, ln_mlp_b):
    nb, seq, d = x.shape
    depth = ln_mix_g.shape[0]
    assert depth == 2 and w_in_ab.shape[0] == 1 and w_in_c.shape[0] == 1
    alpha = (2.0 * depth) ** 0.25
    s5w = s5_w_glu.shape[1]
    sbw = (w_in_ab.shape[2] - s5w) // 3
    hgw = w_in_c.shape[2] // 4
    ltok = N_META + seq
    lp = -(-ltok // SEQ_ALIGN) * SEQ_ALIGN
    rows = nb * lp
    row2 = lambda a: a.reshape(1, -1).astype(F32)

    h = jnp.concatenate([jnp.broadcast_to(meta.astype(x.dtype)[None], (nb, N_META, d)), x,
                         jnp.zeros((nb, lp - ltok, d), x.dtype)], axis=1).reshape(rows, d)

    u, qkv = _inproj_ab(h, w_in_ab[0].astype(BF16), s5w, sbw)
    wb, wc, a_re, a_im = _s5_params(s5_lam_re[0], s5_lam_im[0], s5_log_dt[0], s5_b_re[0],
                                    s5_b_im[0], s5_c_re[0], s5_c_im[0])
    a_out = _s5(u.reshape(nb, lp, s5w), wb, a_re, a_im, wc, row2(s5_d[0]),
                s5_w_glu[0].astype(BF16), row2(s5_b_glu[0]))
    b_out = _stick_breaking(qkv.reshape(nb, lp, 3 * sbw), sbw)
    w_out = w_out_ab[0].astype(BF16)
    h = _block_tail(h, [a_out.reshape(rows, s5w), b_out.reshape(rows, sbw)],
                    [w_out[:s5w], w_out[s5w:]],
                    row2(ln_mix_g[0]), row2(ln_mix_b[0]), mlp_w_up[0].astype(BF16),
                    row2(mlp_b_up[0]), mlp_w_down[0].astype(BF16), row2(mlp_b_down[0]),
                    row2(ln_mlp_g[0]), row2(ln_mlp_b[0]), alpha)

    proj = _inproj_c(h, w_in_c[0].astype(BF16))
    p = jax.nn.softmax(hgrn_gamma.astype(F32), axis=0)
    lb = (jnp.cumsum(p, axis=0) - p[0])[1]
    c_out = _hgrn2(proj.reshape(nb, lp, 4 * hgw), row2(lb), row2(hgrn_norm_g[0]), hgw)
    h = _block_tail(h, [c_out.reshape(rows, hgw)], [w_out_c[0].astype(BF16)],
                    row2(ln_mix_g[1]), row2(ln_mix_b[1]), mlp_w_up[1].astype(BF16),
                    row2(mlp_b_up[1]), mlp_w_down[1].astype(BF16), row2(mlp_b_down[1]),
                    row2(ln_mlp_g[1]), row2(ln_mlp_b[1]), alpha, window=(nb, lp, N_META, seq))
    return h.reshape(nb, seq, d)
```

```python
import functools
import math

import jax
import jax.numpy as jnp
from jax import lax
from jax.experimental import pallas as pl
from jax.experimental.pallas import tpu as pltpu

F32 = jnp.float32
BF16 = jnp.bfloat16

N_META = 16
S5_GROUP = 16
S5_STATE = 64
SB_HEAD_DIM = 64
HG_DK = 128
HG_CHUNK = 64
LN_EPS = 1e-5
RMS_EPS = 1e-6

LANES = 128
SEQ_ALIGN = 64
SB_TILE = 256
ROW_TILE = 512
S5_TIME_CHUNK = SEQ_ALIGN
S5_SLAB_GROUP = 4
FF_CHUNKS = 4
VMEM_LIMIT = 56 * 1024 * 1024


def _row_tile(rows, target):
    best = 8
    for t in range(8, min(rows, target) + 1, 8):
        if rows % t == 0:
            best = t
    return best


def _const_spec(shape):
    nd = len(shape)
    return pl.BlockSpec(shape, lambda *_: (0,) * nd, pipeline_mode=pl.Buffered(1))


def _nt_dot(a, b):
    return lax.dot_general(a, b, (((1,), (1,)), ((), ())), preferred_element_type=F32)


def _split_bf16(x):
    hi = x.astype(BF16)
    lo = (x - hi.astype(F32)).astype(BF16)
    return hi, lo


def _layer_norm(x, g, b):
    mu = jnp.mean(x, axis=-1, keepdims=True)
    xc = x - mu
    var = jnp.mean(xc * xc, axis=-1, keepdims=True)
    return xc * lax.rsqrt(var + LN_EPS) * g + b


def _sigmoid(x):
    return 1.0 / (1.0 + jnp.exp(-x))


def _inproj_ab_kernel(h_ref, w_ref, u_ref, qkv_ref, *, s5w, sbw):
    h = h_ref[...].astype(BF16)
    u_ref[...] = jnp.dot(h, w_ref[:, :s5w], preferred_element_type=F32)
    q = jnp.dot(h, w_ref[:, s5w:s5w + sbw], preferred_element_type=F32)
    qkv_ref[:, :sbw] = (q * (SB_HEAD_DIM ** -0.5)).astype(BF16)
    kv = jnp.dot(h, w_ref[:, s5w + sbw:], preferred_element_type=F32)
    qkv_ref[:, sbw:] = kv.astype(BF16)


def _inproj_ab(h, w, s5w, sbw):
    rows, d = h.shape
    tm = _row_tile(rows, ROW_TILE)
    return pl.pallas_call(
        functools.partial(_inproj_ab_kernel, s5w=s5w, sbw=sbw),
        grid=(rows // tm,),
        in_specs=[pl.BlockSpec((tm, d), lambda i: (i, 0)), _const_spec(w.shape)],
        out_specs=[pl.BlockSpec((tm, s5w), lambda i: (i, 0)),
                   pl.BlockSpec((tm, 3 * sbw), lambda i: (i, 0))],
        out_shape=[jax.ShapeDtypeStruct((rows, s5w), F32),
                   jax.ShapeDtypeStruct((rows, 3 * sbw), BF16)],
        compiler_params=pltpu.CompilerParams(
            dimension_semantics=("parallel",), vmem_limit_bytes=VMEM_LIMIT),
        name="inproj_ab",
    )(h, w)


def _inproj_c_kernel(h_ref, w_ref, o_ref, *, n_split):
    h = h_ref[...].astype(BF16)
    n = w_ref.shape[1] // n_split
    for s in range(n_split):
        o_ref[:, s * n:(s + 1) * n] = jnp.dot(
            h, w_ref[:, s * n:(s + 1) * n], preferred_element_type=F32)


def _inproj_c(h, w):
    rows, d = h.shape
    n = w.shape[1]
    tm = _row_tile(rows, ROW_TILE)
    return pl.pallas_call(
        functools.partial(_inproj_c_kernel, n_split=FF_CHUNKS),
        grid=(rows // tm,),
        in_specs=[pl.BlockSpec((tm, d), lambda i: (i, 0)), _const_spec(w.shape)],
        out_specs=pl.BlockSpec((tm, n), lambda i: (i, 0)),
        out_shape=jax.ShapeDtypeStruct((rows, n), F32),
        compiler_params=pltpu.CompilerParams(
            dimension_semantics=("parallel",), vmem_limit_bytes=VMEM_LIMIT),
        name="inproj_c",
    )(h, w)


S5_SCAN_UNROLL = 2


def _s5_kernel(u_ref, perm_ref, wb_ref, are_ref, aim_ref, wc_ref, d_ref, wglu_ref, bglu_ref,
               o_ref, bu_ref, sre_ref, sim_ref, ytm_ref, y_ref, *, nb, tc, slab_group):
    width = u_ref.shape[2]
    sc = are_ref.shape[1]
    nq = width // LANES
    per_q = sc // nq

    @pl.when(pl.program_id(0) == 0)
    def _():
        sre_ref[...] = jnp.zeros_like(sre_ref)
        sim_ref[...] = jnp.zeros_like(sim_ref)

    n_slabs = sc // LANES
    slabs_q = per_q // LANES
    u = u_ref[...].reshape(nb * tc, width)
    u_tm = jnp.dot(perm_ref[...], u.astype(BF16), preferred_element_type=F32).astype(BF16)
    for q in range(nq):
        r = jnp.dot(u_tm[:, q * LANES:(q + 1) * LANES], wb_ref[q], preferred_element_type=F32)
        for j in range(slabs_q):
            bu_ref[q * slabs_q + j] = r[:, j * LANES:(j + 1) * LANES]
            bu_ref[n_slabs + q * slabs_q + j] = r[:, per_q + j * LANES:per_q + (j + 1) * LANES]

    for s0 in range(0, n_slabs, slab_group):
        slabs = list(range(s0, s0 + slab_group))
        a_re = [jnp.broadcast_to(are_ref[:, s * LANES:(s + 1) * LANES], (nb, LANES)) for s in slabs]
        a_im = [jnp.broadcast_to(aim_ref[:, s * LANES:(s + 1) * LANES], (nb, LANES)) for s in slabs]
        init = []
        for s in slabs:
            init += [sre_ref[:, s * LANES:(s + 1) * LANES], sim_ref[:, s * LANES:(s + 1) * LANES]]

        def body(t, carry, slabs=slabs, a_re=a_re, a_im=a_im):
            new = []
            rows = pl.ds(pl.multiple_of(t * nb, nb), nb)
            for i, s in enumerate(slabs):
                sr, si = carry[2 * i], carry[2 * i + 1]
                nr = a_re[i] * sr - a_im[i] * si + bu_ref[s, rows, :]
                ni = a_re[i] * si + a_im[i] * sr + bu_ref[n_slabs + s, rows, :]
                bu_ref[s, rows, :] = nr
                bu_ref[n_slabs + s, rows, :] = ni
                new += [nr, ni]
            return tuple(new)

        fin = lax.fori_loop(0, tc, body, tuple(init), unroll=S5_SCAN_UNROLL)
        for i, s in enumerate(slabs):
            sre_ref[:, s * LANES:(s + 1) * LANES] = fin[2 * i]
            sim_ref[:, s * LANES:(s + 1) * LANES] = fin[2 * i + 1]

    for q in range(nq):
        s_re = jnp.concatenate(
            [bu_ref[q * slabs_q + j].astype(BF16) for j in range(slabs_q)], axis=1)
        s_im = jnp.concatenate(
            [bu_ref[n_slabs + q * slabs_q + j].astype(BF16) for j in range(slabs_q)], axis=1)
        ytm_ref[q] = (jnp.dot(s_re, wc_ref[q, :per_q, :], preferred_element_type=F32)
                      + jnp.dot(s_im, wc_ref[q, per_q:, :], preferred_element_type=F32))
    for q in range(nq):
        for b in range(nb):
            y_ref[b * tc:(b + 1) * tc, q * LANES:(q + 1) * LANES] = (
                ytm_ref[q, pl.ds(b, tc, stride=nb), :])
    y = y_ref[...] + d_ref[...] * u
    y = 0.5 * y * (1.0 + jnp.tanh(math.sqrt(2.0 / math.pi) * (y + 0.044715 * (y * y * y))))
    gate =jnp.dot(y.astype(BF16), wglu_ref[...], preferred_element_type=F32) + bglu_ref[...]
    o_ref[...] = (y * _sigmoid(gate)).reshape(nb, tc, width).astype(o_ref.dtype)


def _s5_params(lam_re, lam_im, log_dt, b_re, b_im, c_re, c_im):
    g, p = lam_re.shape
    lr, li = lam_re.astype(F32), lam_im.astype(F32)
    dt = jnp.exp(log_dt.astype(F32))[:, None]
    mag = jnp.exp(lr * dt)
    a_re = mag * jnp.cos(li * dt)
    a_im = mag * jnp.sin(li * dt)
    den = lr * lr + li * li
    c_re_ = ((a_re - 1.0) * lr + a_im * li) / den
    c_im_ = (a_im * lr - (a_re - 1.0) * li) / den
    bb_re = c_re_[:, :, None] * b_re.astype(F32) - c_im_[:, :, None] * b_im.astype(F32)
    bb_im = c_re_[:, :, None] * b_im.astype(F32) + c_im_[:, :, None] * b_re.astype(F32)
    gq = LANES // S5_GROUP
    nq = g // gq
    eye = jnp.eye(gq, dtype=F32)

    def b_slab(x):
        x = x.reshape(nq, gq, p, S5_GROUP)
        return jnp.einsum('qgph,gk->qghkp', x, eye).reshape(nq, gq * S5_GROUP, gq * p)

    def c_slab(x):
        x = x.reshape(nq, gq, S5_GROUP, p)
        return jnp.einsum('qghp,gk->qgpkh', x, eye).reshape(nq, gq * p, gq * S5_GROUP)

    wb = jnp.concatenate([b_slab(bb_re), b_slab(bb_im)], axis=2)
    wc = jnp.concatenate([c_slab(c_re.astype(F32)), c_slab(-c_im.astype(F32))], axis=1)
    return wb.astype(BF16), wc.astype(BF16), a_re.reshape(1, g * p), a_im.reshape(1, g * p)


def _s5(u3, wb, a_re, a_im, wc, d, wglu, bglu):
    nb, lp, width = u3.shape
    sc = a_re.shape[1]
    tc = S5_TIME_CHUNK
    assert lp % tc == 0
    rows = nb * tc
    r = jnp.arange(rows)
    perm = (jnp.arange(rows)[None, :] == ((r % nb) * tc + r // nb)[:, None]).astype(BF16)
    return pl.pallas_call(
        functools.partial(_s5_kernel, nb=nb, tc=tc, slab_group=S5_SLAB_GROUP),
        grid=(lp // tc,),
        in_specs=[pl.BlockSpec((nb, tc, width), lambda t: (0, t, 0)), _const_spec(perm.shape),
                  _const_spec(wb.shape), _const_spec(a_re.shape), _const_spec(a_im.shape),
                  _const_spec(wc.shape), _const_spec(d.shape), _const_spec(wglu.shape),
                  _const_spec(bglu.shape)],
        out_specs=pl.BlockSpec((nb, tc, width), lambda t: (0, t, 0)),
        out_shape=jax.ShapeDtypeStruct((nb, lp, width), BF16),
        scratch_shapes=[pltpu.VMEM((2 * sc // LANES, rows, LANES), F32),
                        pltpu.VMEM((nb, sc), F32), pltpu.VMEM((nb, sc), F32),
                        pltpu.VMEM((width // LANES, rows, LANES), F32),
                        pltpu.VMEM((rows, width), F32)],
        compiler_params=pltpu.CompilerParams(
            dimension_semantics=("arbitrary",), vmem_limit_bytes=VMEM_LIMIT),
        name="s5_mixer",
    )(u3, perm, wb, a_re, a_im, wc, d, wglu, bglu)


MASKED_LOG = -1e30
SB_UNROLL = 4
SB_LEAD = 3
SB_TRAIL = SB_LEAD + SB_UNROLL - 1


def _sb_kernel(tbl_ref, q_ref, k_ref, v_ref, o_ref, q_scr, km_ref, vm_ref, carry_scr, acc_scr,
               tri_scr, *bufs, tq, kb, n_diag, n_off, n_tail):
    lp = k_ref.shape[1]
    n_pairs = k_ref.shape[2] // LANES
    n_q = km_ref.shape[0] // n_pairs
    n_heads = LANES // SB_HEAD_DIM
    reps = kb // LANES

    lane =lax.broadcasted_iota(jnp.int32, (kb, LANES), 1)
    for pair in range(n_pairs):
        cols = slice(pair * LANES, (pair + 1) * LANES)
        q0 = pair * n_q * tq
        q_scr[q0:q0 + lp, :] = q_ref[0, :, cols]
        if n_q * tq > lp:
            q_scr[q0 + lp:q0 + n_q * tq, :] = jnp.zeros((n_q * tq - lp, LANES), q_scr.dtype)
        for m in range(n_q):
            valid = min(kb, lp - m * kb)
            k_blk = k_ref[0, m * kb:m * kb + valid, cols]
            v_blk = v_ref[0, m * kb:m * kb + valid, cols]
            if valid < kb:
                pad = jnp.zeros((kb - valid, LANES), k_blk.dtype)
                k_blk = jnp.concatenate([k_blk, pad], axis=0)
                v_blk = jnp.concatenate([v_blk, pad], axis=0)
            for h in range(n_heads):
                in_head = (lane >= h * SB_HEAD_DIM) & (lane < (h + 1) * SB_HEAD_DIM)
                km_ref[pair * n_q + m, h] = jnp.where(in_head, k_blk, jnp.zeros_like(k_blk))
                vm_ref[pair * n_q + m, h] = jnp.where(in_head, v_blk, jnp.zeros_like(v_blk))

    row = lax.broadcasted_iota(jnp.int32, (kb, kb), 0)
    col = lax.broadcasted_iota(jnp.int32, (kb, kb), 1)
    tri = jnp.where(row > col, 1.0, 0.0).astype(BF16)
    below_diag = lax.broadcasted_iota(jnp.int32, (tq, kb), 1) < lax.broadcasted_iota(
        jnp.int32, (tq, kb), 0)

    def item(j):
        return tbl_ref[3 * j], tbl_ref[3 * j + 1], tbl_ref[3 * j + 2]

    z_buf, hl_buf, base_buf, rs_buf, w_buf = (bufs[0:2], bufs[2:4], bufs[4:6], bufs[6:8],
                                              bufs[8:10])
    tri_scr[...] = tri
    for ref in (z_buf[0], hl_buf[1], base_buf[1], rs_buf[1], w_buf[0]):
        ref[...] = jnp.zeros_like(ref)

    def scores(j, z_ref, tm):
        qt, m, _ = item(j)
        q = q_scr[pl.ds(pl.multiple_of(qt * tq, tq), tm), :]
        for h in range(n_heads):
            z_ref[h, :tm] = _nt_dot(q, km_ref[m, h])

    def log_terms(h, z_ref, hl_ref, base_ref, rs_ref, diagonal, tm):
        z = z_ref[h, :tm]
        nlk = jnp.maximum(z, 0.0) + jnp.log(1.0 + jnp.exp(-jnp.abs(z)))
        base = z - nlk
        if diagonal:
            nlk = jnp.where(below_diag, nlk, 0.0)
            base = jnp.where(below_diag, base, MASKED_LOG)
        hl_ref[h, :tm] = nlk.astype(BF16)
        base_ref[h, :tm] = base
        rs_ref[h, :tm] = jnp.broadcast_to(jnp.sum(nlk, axis=1, keepdims=True), (tm, LANES))

    def suffix_sum(h, j, hl_ref, tm, first):
        _, _, slot = item(j)
        inside = jnp.dot(hl_ref[h, :tm], tri_scr[...], preferred_element_type=F32)
        if first:
            return inside
        return inside + jnp.concatenate([carry_scr[slot, h, :tm]] * reps, axis=1)

    def weights(h, j, later, base_ref, rs_ref, w_ref, tm, first):
        _, _, slot = item(j)
        w_ref[h, :tm] = jnp.exp(base_ref[h, :tm] - later).astype(BF16)
        if first:
            carry_scr[slot, h, :tm] = rs_ref[h, :tm]
        else:
            carry_scr[slot, h, :tm] = carry_scr[slot, h, :tm] + rs_ref[h, :tm]

    def weighted_values(j, w_ref, tm, first):
        _, m, slot = item(j)
        pv = jnp.dot(w_ref[0, :tm], vm_ref[m, 0], preferred_element_type=F32)
        for h in range(1, n_heads):
            pv = pv + jnp.dot(w_ref[h, :tm], vm_ref[m, h], preferred_element_type=F32)
        acc_scr[slot, :tm] = pv if first else acc_scr[slot, :tm] + pv

    def run(first, n_real, diagonal, tm):
        n_iter = -(-(n_real + SB_LEAD) // SB_UNROLL) * SB_UNROLL

        def step(i, p):
            j = first + i
            later = []
            for h in range(n_heads):
                log_terms(h, z_buf[p], hl_buf[p], base_buf[p], rs_buf[p], diagonal, tm)
                if h == 0:
                    weighted_values(j - 2, w_buf[p], tm, diagonal)
                later.append(suffix_sum(h, j - 1, hl_buf[1 - p], tm, diagonal))
            for h in range(n_heads):
                weights(h, j - 1, later[h], base_buf[1 - p], rs_buf[1 - p], w_buf[1 - p], tm,
                        diagonal)
                if h == 0:
                    scores(j + 1, z_buf[1 - p], tm)

        def body(it, c):
            for k in range(SB_UNROLL):
                step(SB_UNROLL * it + k, k % 2)
            return c

        lax.fori_loop(0, n_iter // SB_UNROLL, body, 0)

    tail = lp - (n_q - 1) * tq
    start = 0
    for n_items, diagonal, tm in ((n_diag, True, tq), (n_off, False, tq), (n_tail, False, tail)):
        if n_items:
            run(start + SB_LEAD - 1, n_items, diagonal, tm)
        start += SB_LEAD + n_items + SB_TRAIL
    for pair in range(n_pairs):
        o_ref[0, :, pair * LANES:(pair + 1) * LANES] = acc_scr[
            pair * n_q:(pair + 1) * n_q].reshape(n_q * tq, LANES)[:lp].astype(o_ref.dtype)


def _sb_items(n_q, n_pairs, partial_last):
    dummy = [(0, 0, n_pairs * n_q)]
    base = [p * n_q for p in range(n_pairs)]
    n_full = n_q - 1 if partial_last else n_q
    diag = [(o + t, o + t, o + t) for o in base for t in range(n_q)]
    off = [(o + t, o + m, o + t) for o in base for t in range(n_full) for m in range(t - 1, -1, -1)]
    tail = [(o + t, o + m, o + t) for o in base for t in range(n_full, n_q)
            for m in range(t - 1, -1, -1)]
    flat = []
    for items in (diag, off, tail):
        flat += SB_LEAD * dummy + items + SB_TRAIL * dummy
    counts = (len(diag), len(off), len(tail))
    return counts, jnp.asarray([x for e in flat for x in e], jnp.int32)


SB_PAIRS_PER_STEP = 4


def _stick_breaking(qkv3, sbw):
    nb, lp, _ = qkv3.shape
    tq = kb = SB_TILE
    n_heads = LANES // SB_HEAD_DIM
    pps = math.gcd(SB_PAIRS_PER_STEP, sbw // LANES)
    n_steps = sbw // (pps * LANES)
    bw = pps * LANES
    n_q = -(-lp // tq)
    (n_diag, n_off, n_tail), table = _sb_items(n_q, pps, partial_last=lp % tq != 0)
    masked_kv = pltpu.VMEM((pps * n_q, n_heads, kb, LANES), BF16)
    grid_spec = pltpu.PrefetchScalarGridSpec(
        num_scalar_prefetch=1,
        grid=(nb, n_steps),
        in_specs=[pl.BlockSpec((1, lp, bw), lambda b, p, t: (b, 0, p)),
                  pl.BlockSpec((1, lp, bw), lambda b, p, t: (b, 0, n_steps + p)),
                  pl.BlockSpec((1, lp, bw), lambda b, p, t: (b, 0, 2 * n_steps + p))],
        out_specs=pl.BlockSpec((1, lp, bw), lambda b, p, t: (b, 0, p)),
        scratch_shapes=[pltpu.VMEM((pps * n_q * tq, LANES), BF16), masked_kv, masked_kv,
                        pltpu.VMEM((pps * n_q + 1, n_heads, tq, LANES), F32),
                        pltpu.VMEM((pps * n_q + 1, tq, LANES), F32),
                        pltpu.VMEM((kb, kb), BF16)]
        + 2 * [pltpu.VMEM((n_heads, tq, kb), F32)]
        + 2 * [pltpu.VMEM((n_heads, tq, kb), BF16)]
        + 2 * [pltpu.VMEM((n_heads, tq, kb), F32)]
        + 2 * [pltpu.VMEM((n_heads, tq, LANES), F32)]
        + 2 * [pltpu.VMEM((n_heads, tq, kb), BF16)])
    return pl.pallas_call(
        functools.partial(_sb_kernel, tq=tq, kb=kb, n_diag=n_diag, n_off=n_off, n_tail=n_tail),
        grid_spec=grid_spec,
        out_shape=jax.ShapeDtypeStruct((nb, lp, sbw), BF16),
        compiler_params=pltpu.CompilerParams(
            dimension_semantics=("parallel", "parallel"), vmem_limit_bytes=VMEM_LIMIT),
        name="stick_breaking",
    )(table, qkv3, qkv3, qkv3)


def _hgrn_kernel(q_ref, f_ref, v_ref, g_ref, lb_ref, ng_ref, o_ref, *, chunk, group):
    lp = q_ref.shape[1]
    rows = chunk * group
    heads = q_ref.shape[2] // HG_DK
    assert heads == 2
    row = lax.broadcasted_iota(jnp.int32, (chunk, chunk), 0)
    col = lax.broadcasted_iota(jnp.int32, (chunk, chunk), 1)
    causal = col <= row
    tri = jnp.where(causal, 1.0, 0.0).astype(BF16)
    tri2 = jnp.concatenate([tri, tri], axis=1)
    col2 = lax.broadcasted_iota(jnp.int32, (chunk, heads * chunk), 1)
    row2 = lax.broadcasted_iota(jnp.int32, (chunk, heads * chunk), 0)
    causal2 = jnp.where(col2 >= chunk, col2 - chunk, col2) <= row2
    first = lax.broadcasted_iota(jnp.int32, (chunk, heads * HG_DK), 1) < HG_DK
    zero_state = jnp.zeros((HG_DK, HG_DK), BF16)
    lb = lb_ref[...]
    hs = [slice(h * HG_DK, (h + 1) * HG_DK) for h in range(heads)]

    def block_diag(x):
        zero = jnp.zeros_like(x)
        return jnp.concatenate([jnp.where(first, x, zero), jnp.where(first, zero, x)], axis=0)

    def body(s, states):
        sl = pl.ds(pl.multiple_of(s * rows, rows), rows)
        f = lb + (1.0 - lb) * _sigmoid(f_ref[0, sl, :])
        kk = 1.0 - f
        hi, lo = _split_bf16(jnp.log(f))
        cs = [slice(c * chunk, (c + 1) * chunk) for c in range(group)]
        bcum = jnp.concatenate(
            [jnp.dot(tri2, jnp.concatenate([hi[r], lo[r]], axis=0), preferred_element_type=F32)
             for r in cs], axis=0)
        q_dec = (q_ref[0, sl, :] * jnp.exp(bcum)).astype(BF16)
        k_inv_f = kk * jnp.exp(-bcum)
        k_inv = k_inv_f.astype(BF16)
        v = v_ref[0, sl, :].astype(BF16)
        states = list(states)
        outs = []
        for r in cs:
            e_last = jnp.exp(bcum[r.stop - 1:r.stop, :])
            k_end = (k_inv_f[r] * e_last).astype(BF16)
            scores = jnp.where(causal2, _nt_dot(q_dec[r], block_diag(k_inv[r])), 0.0)
            st_bd = jnp.concatenate(
                [jnp.concatenate([states[0].astype(BF16), zero_state], axis=1),
                 jnp.concatenate([zero_state, states[1].astype(BF16)], axis=1)], axis=0)
            outs.append(jnp.dot(scores.astype(BF16), block_diag(v[r]), preferred_element_type=F32)
                        + _nt_dot(q_dec[r], st_bd))
            for h in range(heads):
                states[h] = states[h] * e_last[:, hs[h]] + lax.dot_general(
                    v[r][:, hs[h]], k_end[:, hs[h]], (((0,), (0,)), ((), ())),
                    preferred_element_type=F32)
        o = jnp.concatenate(outs, axis=0)
        o = jnp.concatenate(
            [o[:, c] * lax.rsqrt(jnp.mean(o[:, c] * o[:, c], axis=-1, keepdims=True) + RMS_EPS)
             for c in hs], axis=1)
        g = g_ref[0, sl, :]
        o_ref[0, sl, :] = (o * ng_ref[...] * (g * _sigmoid(g))).astype(o_ref.dtype)
        return tuple(states)

    init = tuple(jnp.zeros((HG_DK, HG_DK), F32) for _ in range(heads))
    lax.fori_loop(0, lp // rows, body, init)


HG_MAX_GROUP = 17


def _hgrn2(proj3, lb, norm_g, width):
    nb, lp, _ = proj3.shape
    bw = 2 * HG_DK
    n_blk = width // bw
    n_chunks = lp // HG_CHUNK
    group = max(g for g in range(1, HG_MAX_GROUP + 1) if n_chunks % g == 0)

    def col_spec(part):
        return pl.BlockSpec((1, lp, bw), lambda b, j, part=part: (b, 0, part * n_blk + j))

    vec_spec = pl.BlockSpec((1, bw), lambda b, j: (0, j))
    return pl.pallas_call(
        functools.partial(_hgrn_kernel, chunk=HG_CHUNK, group=group),
        grid=(nb, n_blk),
        in_specs=[col_spec(0), col_spec(1), col_spec(2), col_spec(3), vec_spec, vec_spec],
        out_specs=pl.BlockSpec((1, lp, bw), lambda b, j: (b, 0, j)),
        out_shape=jax.ShapeDtypeStruct((nb, lp, width), BF16),
        compiler_params=pltpu.CompilerParams(
            dimension_semantics=("parallel", "parallel"), vmem_limit_bytes=VMEM_LIMIT),
        name="hgrn2_mixer",
    )(proj3, proj3, proj3, proj3, lb, norm_g)


def _block_tail_kernel(*refs, n_mix, alpha, ff_chunks):
    h_ref = refs[0]
    mix_refs = refs[1:1 + n_mix]
    w_refs = refs[1 + n_mix:1 + 2 * n_mix]
    (g1_ref, b1_ref, wup_ref, bup_ref, wdn_ref, bdn_ref, g2_ref, b2_ref, o_ref) = refs[1 + 2 * n_mix:]
    mix = jnp.dot(mix_refs[0][...], w_refs[0][...], preferred_element_type=F32)
    for m_ref, w_ref in zip(mix_refs[1:], w_refs[1:]):
        mix = mix + jnp.dot(m_ref[...], w_ref[...], preferred_element_type=F32)
    h1 = _layer_norm(alpha * h_ref[...] + mix, g1_ref[...], b1_ref[...])
    h1b = h1.astype(BF16)
    dff = wup_ref.shape[1]
    fc = dff // ff_chunks
    acc = alpha * h1 + bdn_ref[...]
    for c in range(ff_chunks):
        hid = jnp.dot(h1b, wup_ref[:, c * fc:(c + 1) * fc], preferred_element_type=F32)
        hid = jnp.maximum(hid + bup_ref[:, c * fc:(c + 1) * fc], 0.0)
        hid = (hid * hid).astype(BF16)
        acc = acc + jnp.dot(hid, wdn_ref[c * fc:(c + 1) * fc, :], preferred_element_type=F32)
    o_ref[...] = _layer_norm(acc, g2_ref[...], b2_ref[...])


def _block_tail(h, mixes, ws, g1, b1, wup, bup, wdn, bdn, g2, b2, alpha, window=None):
    rows, d = h.shape
    consts = [g1, b1, wup, bup, wdn, bdn, g2, b2]
    if window is None:
        tm = _row_tile(rows, ROW_TILE)
        grid = (rows // tm,)
        row_spec = lambda n: pl.BlockSpec((tm, n), lambda i: (i, 0))
        out_spec, out_rows = row_spec(d), rows
    else:
        nb, lp, start, count = window
        tm = _row_tile(count, ROW_TILE)
        per_seq = count // tm
        grid = (nb, per_seq)
        align = math.gcd(lp, start, tm)
        row_spec = lambda n: pl.BlockSpec(
            (pl.Element(tm), pl.Element(n)),
            lambda b, j: (pl.multiple_of(b * lp + start + j * tm, align), 0))
        out_spec = pl.BlockSpec((tm, d), lambda b, j: (b * per_seq + j, 0))
        out_rows = nb * count
    return pl.pallas_call(
        functools.partial(_block_tail_kernel, n_mix=len(mixes), alpha=alpha,
                          ff_chunks=FF_CHUNKS),
        grid=grid,
        in_specs=([row_spec(d)] + [row_spec(m.shape[1]) for m in mixes]
                  + [_const_spec(w.shape) for w in ws] + [_const_spec(c.shape) for c in consts]),
        out_specs=out_spec,
        out_shape=jax.ShapeDtypeStruct((out_rows, d), F32),
        compiler_params=pltpu.CompilerParams(
            dimension_semantics=("parallel",) * len(grid), vmem_limit_bytes=VMEM_LIMIT),
        name="block_tail",
    )(h, *mixes, *ws, *consts)


def kernel(x, meta, w_in_ab, s5_lam_re, s5_lam_im, s5_log_dt, s5_b_re, s5_b_im, s5_c_re, s5_c_im, s5_d, s5_w_glu, s5_b_glu, w_out_ab, w_in_c, hgrn_gamma, hgrn_norm_g, w_out_c, ln_mix_g, ln_mix_b, mlp_w_up, mlp_b_up, mlp_w_down, mlp_b_down, ln_mlp_g, ln_mlp_b):
    nb, seq, d = x.shape
    depth = ln_mix_g.shape[0]
    assert depth == 2 and w_in_ab.shape[0] == 1 and w_in_c.shape[0] == 1
    alpha = (2.0 * depth) ** 0.25
    s5w = s5_w_glu.shape[1]
    sbw = (w_in_ab.shape[2] - s5w) // 3
    hgw = w_in_c.shape[2] // 4
    ltok = N_META + seq
    lp = -(-ltok // SEQ_ALIGN) * SEQ_ALIGN
    rows = nb * lp
    row2 = lambda a: a.reshape(1, -1).astype(F32)

    h = jnp.concatenate([jnp.broadcast_to(meta.astype(x.dtype)[None], (nb, N_META, d)), x,
                         jnp.zeros((nb, lp - ltok, d), x.dtype)], axis=1).reshape(rows, d)

    u, qkv = _inproj_ab(h, w_in_ab[0].astype(BF16), s5w, sbw)
    wb, wc, a_re, a_im = _s5_params(s5_lam_re[0], s5_lam_im[0], s5_log_dt[0], s5_b_re[0],
                                    s5_b_im[0], s5_c_re[0], s5_c_im[0])
    a_out = _s5(u.reshape(nb, lp, s5w), wb, a_re, a_im, wc, row2(s5_d[0]),
                s5_w_glu[0].astype(BF16), row2(s5_b_glu[0]))
    b_out = _stick_breaking(qkv.reshape(nb, lp, 3 * sbw), sbw)
    w_out = w_out_ab[0].astype(BF16)
    h = _block_tail(h, [a_out.reshape(rows, s5w), b_out.reshape(rows, sbw)],
                    [w_out[:s5w], w_out[s5w:]],
                    row2(ln_mix_g[0]), row2(ln_mix_b[0]), mlp_w_up[0].astype(BF16),
                    row2(mlp_b_up[0]), mlp_w_down[0].astype(BF16), row2(mlp_b_down[0]),
                    row2(ln_mlp_g[0]), row2(ln_mlp_b[0]), alpha)

    proj = _inproj_c(h, w_in_c[0].astype(BF16))
    p = jax.nn.softmax(hgrn_gamma.astype(F32), axis=0)
    lb = (jnp.cumsum(p, axis=0) - p[0])[1]
    c_out = _hgrn2(proj.reshape(nb, lp, 4 * hgw), row2(lb), row2(hgrn_norm_g[0]), hgw)
    h = _block_tail(h, [c_out.reshape(rows, hgw)], [w_out_c[0].astype(BF16)],
                    row2(ln_mix_g[1]), row2(ln_mix_b[1]), mlp_w_up[1].astype(BF16),
                    row2(mlp_b_up[1]), mlp_w_down[1].astype(BF16), row2(mlp_b_down[1]),
                    row2(ln_mlp_g[1]), row2(ln_mlp_b[1]), alpha, window=(nb, lp, N_META, seq))
    return h.reshape(nb, seq, d)
```

```python
import functools
import math

import jax
import jax.numpy as jnp
from jax import lax
from jax.experimental import pallas as pl
from jax.experimental.pallas import tpu as pltpu

F32 = jnp.float32
BF16 = jnp.bfloat16

N_META = 16
S5_GROUP = 16
S5_STATE = 64
SB_HEAD_DIM = 64
HG_DK = 128
HG_CHUNK = 64
LN_EPS = 1e-5
RMS_EPS = 1e-6

LANES = 128
SEQ_ALIGN = 64
SB_TILE = 256
ROW_TILE = 1024
S5_TIME_CHUNK = SEQ_ALIGN
S5_SLAB_GROUP = 4
FF_CHUNKS = 4
VMEM_LIMIT = 56 * 1024 * 1024


def _row_tile(rows, target):
    best = 8
    for t in range(8, min(rows, target) + 1, 8):
        if rows % t == 0:
            best = t
    return best


def _const_spec(shape):
    nd = len(shape)
    return pl.BlockSpec(shape, lambda *_: (0,) * nd, pipeline_mode=pl.Buffered(1))


def _nt_dot(a, b):
    return lax.dot_general(a, b, (((1,), (1,)), ((), ())), preferred_element_type=F32)


def _split_bf16(x):
    hi = x.astype(BF16)
    lo = (x - hi.astype(F32)).astype(BF16)
    return hi, lo


def _layer_norm(x, g, b):
    mu = jnp.mean(x, axis=-1, keepdims=True)
    xc = x - mu
    var = jnp.mean(xc * xc, axis=-1, keepdims=True)
    return xc * lax.rsqrt(var + LN_EPS) * g + b


def _sigmoid(x):
    return 1.0 / (1.0 + jnp.exp(-x))


def _inproj_ab_kernel(h_ref, w_ref, u_ref, qkv_ref, *, s5w, sbw):
    h = h_ref[...].astype(BF16)
    u_ref[...] = jnp.dot(h, w_ref[:, :s5w], preferred_element_type=F32)
    q = jnp.dot(h, w_ref[:, s5w:s5w + sbw], preferred_element_type=F32)
    qkv_ref[:, :sbw] = (q * (SB_HEAD_DIM ** -0.5)).astype(BF16)
    kv = jnp.dot(h, w_ref[:, s5w + sbw:], preferred_element_type=F32)
    qkv_ref[:, sbw:] = kv.astype(BF16)


def _inproj_ab(h, w, s5w, sbw):
    rows, d = h.shape
    tm = _row_tile(rows, ROW_TILE)
    return pl.pallas_call(
        functools.partial(_inproj_ab_kernel, s5w=s5w, sbw=sbw),
        grid=(rows // tm,),
        in_specs=[pl.BlockSpec((tm, d), lambda i: (i, 0)), _const_spec(w.shape)],
        out_specs=[pl.BlockSpec((tm, s5w), lambda i: (i, 0)),
                   pl.BlockSpec((tm, 3 * sbw), lambda i: (i, 0))],
        out_shape=[jax.ShapeDtypeStruct((rows, s5w), F32),
                   jax.ShapeDtypeStruct((rows, 3 * sbw), BF16)],
        compiler_params=pltpu.CompilerParams(
            dimension_semantics=("parallel",), vmem_limit_bytes=VMEM_LIMIT),
        name="inproj_ab",
    )(h, w)


def _inproj_c_kernel(h_ref, w_ref, o_ref, *, n_split):
    h = h_ref[...].astype(BF16)
    n = w_ref.shape[1] // n_split
    for s in range(n_split):
        o_ref[:, s * n:(s + 1) * n] = jnp.dot(
            h, w_ref[:, s * n:(s + 1) * n], preferred_element_type=F32)


def _inproj_c(h, w):
    rows, d = h.shape
    n = w.shape[1]
    tm = _row_tile(rows, ROW_TILE)
    return pl.pallas_call(
        functools.partial(_inproj_c_kernel, n_split=FF_CHUNKS),
        grid=(rows // tm,),
        in_specs=[pl.BlockSpec((tm, d), lambda i: (i, 0)), _const_spec(w.shape)],
        out_specs=pl.BlockSpec((tm, n), lambda i: (i, 0)),
        out_shape=jax.ShapeDtypeStruct((rows, n), F32),
        compiler_params=pltpu.CompilerParams(
            dimension_semantics=("parallel",), vmem_limit_bytes=VMEM_LIMIT),
        name="inproj_c",
    )(h, w)


S5_SCAN_UNROLL = 2


def _s5_kernel(u_ref, perm_ref, wb_ref, are_ref, aim_ref, wc_ref, d_ref, wglu_ref, bglu_ref,
               o_ref, bu_ref, sre_ref, sim_ref, ytm_ref, y_ref, *, nb, tc, slab_group):
    width = u_ref.shape[2]
    sc = are_ref.shape[1]
    nq = width // LANES
    per_q = sc // nq

    @pl.when(pl.program_id(0) == 0)
    def _():
        sre_ref[...] = jnp.zeros_like(sre_ref)
        sim_ref[...] = jnp.zeros_like(sim_ref)

    n_slabs = sc // LANES
    slabs_q = per_q // LANES
    u = u_ref[...].reshape(nb * tc, width)
    u_tm = jnp.dot(perm_ref[...], u.astype(BF16), preferred_element_type=F32).astype(BF16)
    for q in range(nq):
        r = jnp.dot(u_tm[:, q * LANES:(q + 1) * LANES], wb_ref[q], preferred_element_type=F32)
        for j in range(slabs_q):
            bu_ref[q * slabs_q + j] = r[:, j * LANES:(j + 1) * LANES]
            bu_ref[n_slabs + q * slabs_q + j] = r[:, per_q + j * LANES:per_q + (j + 1) * LANES]

    for s0 in range(0, n_slabs, slab_group):
        slabs = list(range(s0, s0 + slab_group))
        a_re = [jnp.broadcast_to(are_ref[:, s * LANES:(s + 1) * LANES], (nb, LANES)) for s in slabs]
        a_im = [jnp.broadcast_to(aim_ref[:, s * LANES:(s + 1) * LANES], (nb, LANES)) for s in slabs]
        init = []
        for s in slabs:
            init += [sre_ref[:, s * LANES:(s + 1) * LANES], sim_ref[:, s * LANES:(s + 1) * LANES]]

        def body(t, carry, slabs=slabs, a_re=a_re, a_im=a_im):
            new = []
            rows = pl.ds(pl.multiple_of(t * nb, nb), nb)
            for i, s in enumerate(slabs):
                sr, si = carry[2 * i], carry[2 * i + 1]
                nr = a_re[i] * sr - a_im[i] * si + bu_ref[s, rows, :]
                ni = a_re[i] * si + a_im[i] * sr + bu_ref[n_slabs + s, rows, :]
                bu_ref[s, rows, :] = nr
                bu_ref[n_slabs + s, rows, :] = ni
                new += [nr, ni]
            return tuple(new)

        fin = lax.fori_loop(0, tc, body, tuple(init), unroll=S5_SCAN_UNROLL)
        for i, s in enumerate(slabs):
            sre_ref[:, s * LANES:(s + 1) * LANES] = fin[2 * i]
            sim_ref[:, s * LANES:(s + 1) * LANES] = fin[2 * i + 1]

    for q in range(nq):
        s_re = jnp.concatenate(
            [bu_ref[q * slabs_q + j].astype(BF16) for j in range(slabs_q)], axis=1)
        s_im = jnp.concatenate(
            [bu_ref[n_slabs + q * slabs_q + j].astype(BF16) for j in range(slabs_q)], axis=1)
        ytm_ref[q] = (jnp.dot(s_re, wc_ref[q, :per_q, :], preferred_element_type=F32)
                      + jnp.dot(s_im, wc_ref[q, per_q:, :], preferred_element_type=F32))
    for q in range(nq):
        for b in range(nb):
            y_ref[b * tc:(b + 1) * tc, q * LANES:(q + 1) * LANES] = (
                ytm_ref[q, pl.ds(b, tc, stride=nb), :])
    y = y_ref[...] + d_ref[...] * u
    y = 0.5 * y * (1.0 + jnp.tanh(math.sqrt(2.0 / math.pi) * (y + 0.044715 * (y * y * y))))
    gate =jnp.dot(y.astype(BF16), wglu_ref[...], preferred_element_type=F32) + bglu_ref[...]
    o_ref[...] = (y * _sigmoid(gate)).reshape(nb, tc, width).astype(o_ref.dtype)


def _s5_params(lam_re, lam_im, log_dt, b_re, b_im, c_re, c_im):
    g, p = lam_re.shape
    lr, li = lam_re.astype(F32), lam_im.astype(F32)
    dt = jnp.exp(log_dt.astype(F32))[:, None]
    mag = jnp.exp(lr * dt)
    a_re = mag * jnp.cos(li * dt)
    a_im = mag * jnp.sin(li * dt)
    den = lr * lr + li * li
    c_re_ = ((a_re - 1.0) * lr + a_im * li) / den
    c_im_ = (a_im * lr - (a_re - 1.0) * li) / den
    bb_re = c_re_[:, :, None] * b_re.astype(F32) - c_im_[:, :, None] * b_im.astype(F32)
    bb_im = c_re_[:, :, None] * b_im.astype(F32) + c_im_[:, :, None] * b_re.astype(F32)
    gq = LANES // S5_GROUP
    nq = g // gq
    eye = jnp.eye(gq, dtype=F32)

    def b_slab(x):
        x = x.reshape(nq, gq, p, S5_GROUP)
        return jnp.einsum('qgph,gk->qghkp', x, eye).reshape(nq, gq * S5_GROUP, gq * p)

    def c_slab(x):
        x = x.reshape(nq, gq, S5_GROUP, p)
        return jnp.einsum('qghp,gk->qgpkh', x, eye).reshape(nq, gq * p, gq * S5_GROUP)

    wb = jnp.concatenate([b_slab(bb_re), b_slab(bb_im)], axis=2)
    wc = jnp.concatenate([c_slab(c_re.astype(F32)), c_slab(-c_im.astype(F32))], axis=1)
    return wb.astype(BF16), wc.astype(BF16), a_re.reshape(1, g * p), a_im.reshape(1, g * p)


def _s5(u3, wb, a_re, a_im, wc, d, wglu, bglu):
    nb, lp, width = u3.shape
    sc = a_re.shape[1]
    tc = S5_TIME_CHUNK
    assert lp % tc == 0
    rows = nb * tc
    r = jnp.arange(rows)
    perm = (jnp.arange(rows)[None, :] == ((r % nb) * tc + r // nb)[:, None]).astype(BF16)
    return pl.pallas_call(
        functools.partial(_s5_kernel, nb=nb, tc=tc, slab_group=S5_SLAB_GROUP),
        grid=(lp // tc,),
        in_specs=[pl.BlockSpec((nb, tc, width), lambda t: (0, t, 0)), _const_spec(perm.shape),
                  _const_spec(wb.shape), _const_spec(a_re.shape), _const_spec(a_im.shape),
                  _const_spec(wc.shape), _const_spec(d.shape), _const_spec(wglu.shape),
                  _const_spec(bglu.shape)],
        out_specs=pl.BlockSpec((nb, tc, width), lambda t: (0, t, 0)),
        out_shape=jax.ShapeDtypeStruct((nb, lp, width), BF16),
        scratch_shapes=[pltpu.VMEM((2 * sc // LANES, rows, LANES), F32),
                        pltpu.VMEM((nb, sc), F32), pltpu.VMEM((nb, sc), F32),
                        pltpu.VMEM((width // LANES, rows, LANES), F32),
                        pltpu.VMEM((rows, width), F32)],
        compiler_params=pltpu.CompilerParams(
            dimension_semantics=("arbitrary",), vmem_limit_bytes=VMEM_LIMIT),
        name="s5_mixer",
    )(u3, perm, wb, a_re, a_im, wc, d, wglu, bglu)


MASKED_LOG = -1e30
SB_UNROLL = 4
SB_LEAD = 3
SB_TRAIL = SB_LEAD + SB_UNROLL - 1


def _sb_kernel(tbl_ref, q_ref, k_ref, v_ref, o_ref, q_scr, km_ref, vm_ref, carry_scr, acc_scr,
               tri_scr, *bufs, tq, kb, n_diag, n_off, n_tail):
    lp = k_ref.shape[1]
    n_pairs = k_ref.shape[2] // LANES
    n_q = km_ref.shape[0] // n_pairs
    n_heads = LANES // SB_HEAD_DIM
    reps = kb // LANES

    lane =lax.broadcasted_iota(jnp.int32, (kb, LANES), 1)
    for pair in range(n_pairs):
        cols = slice(pair * LANES, (pair + 1) * LANES)
        q0 = pair * n_q * tq
        q_scr[q0:q0 + lp, :] = q_ref[0, :, cols]
        if n_q * tq > lp:
            q_scr[q0 + lp:q0 + n_q * tq, :] = jnp.zeros((n_q * tq - lp, LANES), q_scr.dtype)
        for m in range(n_q):
            valid = min(kb, lp - m * kb)
            k_blk = k_ref[0, m * kb:m * kb + valid, cols]
            v_blk = v_ref[0, m * kb:m * kb + valid, cols]
            if valid < kb:
                pad = jnp.zeros((kb - valid, LANES), k_blk.dtype)
                k_blk = jnp.concatenate([k_blk, pad], axis=0)
                v_blk = jnp.concatenate([v_blk, pad], axis=0)
            for h in range(n_heads):
                in_head = (lane >= h * SB_HEAD_DIM) & (lane < (h + 1) * SB_HEAD_DIM)
                km_ref[pair * n_q + m, h] = jnp.where(in_head, k_blk, jnp.zeros_like(k_blk))
                vm_ref[pair * n_q + m, h] = jnp.where(in_head, v_blk, jnp.zeros_like(v_blk))

    row = lax.broadcasted_iota(jnp.int32, (kb, kb), 0)
    col = lax.broadcasted_iota(jnp.int32, (kb, kb), 1)
    tri = jnp.where(row > col, 1.0, 0.0).astype(BF16)
    below_diag = lax.broadcasted_iota(jnp.int32, (tq, kb), 1) < lax.broadcasted_iota(
        jnp.int32, (tq, kb), 0)

    def item(j):
        return tbl_ref[3 * j], tbl_ref[3 * j + 1], tbl_ref[3 * j + 2]

    z_buf, hl_buf, base_buf, rs_buf, w_buf = (bufs[0:2], bufs[2:4], bufs[4:6], bufs[6:8],
                                              bufs[8:10])
    tri_scr[...] = tri
    for ref in (z_buf[0], hl_buf[1], base_buf[1], rs_buf[1], w_buf[0]):
        ref[...] = jnp.zeros_like(ref)

    def scores(j, z_ref, tm):
        qt, m, _ = item(j)
        q = q_scr[pl.ds(pl.multiple_of(qt * tq, tq), tm), :]
        for h in range(n_heads):
            z_ref[h, :tm] = _nt_dot(q, km_ref[m, h])

    def log_terms(h, z_ref, hl_ref, base_ref, rs_ref, diagonal, tm):
        z = z_ref[h, :tm]
        nlk = jnp.maximum(z, 0.0) + jnp.log(1.0 + jnp.exp(-jnp.abs(z)))
        base = z - nlk
        if diagonal:
            nlk = jnp.where(below_diag, nlk, 0.0)
            base = jnp.where(below_diag, base, MASKED_LOG)
        hl_ref[h, :tm] = nlk.astype(BF16)
        base_ref[h, :tm] = base
        rs_ref[h, :tm] = jnp.broadcast_to(jnp.sum(nlk, axis=1, keepdims=True), (tm, LANES))

    def suffix_sum(h, j, hl_ref, tm, first):
        _, _, slot = item(j)
        inside = jnp.dot(hl_ref[h, :tm], tri_scr[...], preferred_element_type=F32)
        if first:
            return inside
        return inside + jnp.concatenate([carry_scr[slot, h, :tm]] * reps, axis=1)

    def weights(h, j, later, base_ref, rs_ref, w_ref, tm, first):
        _, _, slot = item(j)
        w_ref[h, :tm] = jnp.exp(base_ref[h, :tm] - later).astype(BF16)
        if first:
            carry_scr[slot, h, :tm] = rs_ref[h, :tm]
        else:
            carry_scr[slot, h, :tm] = carry_scr[slot, h, :tm] + rs_ref[h, :tm]

    def weighted_values(j, w_ref, tm, first):
        _, m, slot = item(j)
        pv = jnp.dot(w_ref[0, :tm], vm_ref[m, 0], preferred_element_type=F32)
        for h in range(1, n_heads):
            pv = pv + jnp.dot(w_ref[h, :tm], vm_ref[m, h], preferred_element_type=F32)
        acc_scr[slot, :tm] = pv if first else acc_scr[slot, :tm] + pv

    def run(first, n_real, diagonal, tm):
        n_iter = -(-(n_real + SB_LEAD) // SB_UNROLL) * SB_UNROLL

        def step(i, p):
            j = first + i
            later = []
            for h in range(n_heads):
                log_terms(h, z_buf[p], hl_buf[p], base_buf[p], rs_buf[p], diagonal, tm)
                if h == 0:
                    weighted_values(j - 2, w_buf[p], tm, diagonal)
                later.append(suffix_sum(h, j - 1, hl_buf[1 - p], tm, diagonal))
            for h in range(n_heads):
                weights(h, j - 1, later[h], base_buf[1 - p], rs_buf[1 - p], w_buf[1 - p], tm,
                        diagonal)
                if h == 0:
                    scores(j + 1, z_buf[1 - p], tm)

        def body(it, c):
            for k in range(SB_UNROLL):
                step(SB_UNROLL * it + k, k % 2)
            return c

        lax.fori_loop(0, n_iter // SB_UNROLL, body, 0)

    tail = lp - (n_q - 1) * tq
    start = 0
    for n_items, diagonal, tm in ((n_diag, True, tq), (n_off, False, tq), (n_tail, False, tail)):
        if n_items:
            run(start + SB_LEAD - 1, n_items, diagonal, tm)
        start += SB_LEAD + n_items + SB_TRAIL
    for pair in range(n_pairs):
        o_ref[0, :, pair * LANES:(pair + 1) * LANES] = acc_scr[
            pair * n_q:(pair + 1) * n_q].reshape(n_q * tq, LANES)[:lp].astype(o_ref.dtype)


def _sb_items(n_q, n_pairs, partial_last):
    dummy = [(0, 0, n_pairs * n_q)]
    base = [p * n_q for p in range(n_pairs)]
    n_full = n_q - 1 if partial_last else n_q
    diag = [(o + t, o + t, o + t) for o in base for t in range(n_q)]
    off = [(o + t, o + m, o + t) for o in base for t in range(n_full) for m in range(t - 1, -1, -1)]
    tail = [(o + t, o + m, o + t) for o in base for t in range(n_full, n_q)
            for m in range(t - 1, -1, -1)]
    flat = []
    for items in (diag, off, tail):
        flat += SB_LEAD * dummy + items + SB_TRAIL * dummy
    counts = (len(diag), len(off), len(tail))
    return counts, jnp.asarray([x for e in flat for x in e], jnp.int32)


SB_PAIRS_PER_STEP = 2


def _stick_breaking(qkv3, sbw):
    nb, lp, _ = qkv3.shape
    tq = kb = SB_TILE
    n_heads = LANES // SB_HEAD_DIM
    pps = math.gcd(SB_PAIRS_PER_STEP, sbw // LANES)
    n_steps = sbw // (pps * LANES)
    bw = pps * LANES
    n_q = -(-lp // tq)
    (n_diag, n_off, n_tail), table = _sb_items(n_q, pps, partial_last=lp % tq != 0)
    masked_kv = pltpu.VMEM((pps * n_q, n_heads, kb, LANES), BF16)
    grid_spec = pltpu.PrefetchScalarGridSpec(
        num_scalar_prefetch=1,
        grid=(nb, n_steps),
        in_specs=[pl.BlockSpec((1, lp, bw), lambda b, p, t: (b, 0, p)),
                  pl.BlockSpec((1, lp, bw), lambda b, p, t: (b, 0, n_steps + p)),
                  pl.BlockSpec((1, lp, bw), lambda b, p, t: (b, 0, 2 * n_steps + p))],
        out_specs=pl.BlockSpec((1, lp, bw), lambda b, p, t: (b, 0, p)),
        scratch_shapes=[pltpu.VMEM((pps * n_q * tq, LANES), BF16), masked_kv, masked_kv,
                        pltpu.VMEM((pps * n_q + 1, n_heads, tq, LANES), F32),
                        pltpu.VMEM((pps * n_q + 1, tq, LANES), F32),
                        pltpu.VMEM((kb, kb), BF16)]
        + 2 * [pltpu.VMEM((n_heads, tq, kb), F32)]
        + 2 * [pltpu.VMEM((n_heads, tq, kb), BF16)]
        + 2 * [pltpu.VMEM((n_heads, tq, kb), F32)]
        + 2 * [pltpu.VMEM((n_heads, tq, LANES), F32)]
        + 2 * [pltpu.VMEM((n_heads, tq, kb), BF16)])
    return pl.pallas_call(
        functools.partial(_sb_kernel, tq=tq, kb=kb, n_diag=n_diag, n_off=n_off, n_tail=n_tail),
        grid_spec=grid_spec,
        out_shape=jax.ShapeDtypeStruct((nb, lp, sbw), BF16),
        compiler_params=pltpu.CompilerParams(
            dimension_semantics=("parallel", "parallel"), vmem_limit_bytes=VMEM_LIMIT),
        name="stick_breaking",
    )(table, qkv3, qkv3, qkv3)


def _hgrn_kernel(q_ref, f_ref, v_ref, g_ref, lb_ref, ng_ref, o_ref, *, chunk, group):
    lp = q_ref.shape[1]
    rows = chunk * group
    heads = q_ref.shape[2] // HG_DK
    assert heads == 2
    row = lax.broadcasted_iota(jnp.int32, (chunk, chunk), 0)
    col = lax.broadcasted_iota(jnp.int32, (chunk, chunk), 1)
    causal = col <= row
    tri = jnp.where(causal, 1.0, 0.0).astype(BF16)
    tri2 = jnp.concatenate([tri, tri], axis=1)
    col2 = lax.broadcasted_iota(jnp.int32, (chunk, heads * chunk), 1)
    row2 = lax.broadcasted_iota(jnp.int32, (chunk, heads * chunk), 0)
    causal2 = jnp.where(col2 >= chunk, col2 - chunk, col2) <= row2
    first = lax.broadcasted_iota(jnp.int32, (chunk, heads * HG_DK), 1) < HG_DK
    zero_state = jnp.zeros((HG_DK, HG_DK), BF16)
    lb = lb_ref[...]
    hs = [slice(h * HG_DK, (h + 1) * HG_DK) for h in range(heads)]

    def block_diag(x):
        zero = jnp.zeros_like(x)
        return jnp.concatenate([jnp.where(first, x, zero), jnp.where(first, zero, x)], axis=0)

    def body(s, states):
        sl = pl.ds(pl.multiple_of(s * rows, rows), rows)
        f = lb + (1.0 - lb) * _sigmoid(f_ref[0, sl, :])
        kk = 1.0 - f
        hi, lo = _split_bf16(jnp.log(f))
        cs = [slice(c * chunk, (c + 1) * chunk) for c in range(group)]
        bcum = jnp.concatenate(
            [jnp.dot(tri2, jnp.concatenate([hi[r], lo[r]], axis=0), preferred_element_type=F32)
             for r in cs], axis=0)
        q_dec = (q_ref[0, sl, :] * jnp.exp(bcum)).astype(BF16)
        k_inv_f = kk * jnp.exp(-bcum)
        k_inv = k_inv_f.astype(BF16)
        v = v_ref[0, sl, :].astype(BF16)
        states = list(states)
        outs = []
        for r in cs:
            e_last = jnp.exp(bcum[r.stop - 1:r.stop, :])
            k_end = (k_inv_f[r] * e_last).astype(BF16)
            scores = jnp.where(causal2, _nt_dot(q_dec[r], block_diag(k_inv[r])), 0.0)
            st_bd = jnp.concatenate(
                [jnp.concatenate([states[0].astype(BF16), zero_state], axis=1),
                 jnp.concatenate([zero_state, states[1].astype(BF16)], axis=1)], axis=0)
            outs.append(jnp.dot(scores.astype(BF16), block_diag(v[r]), preferred_element_type=F32)
                        + _nt_dot(q_dec[r], st_bd))
            for h in range(heads):
                states[h] = states[h] * e_last[:, hs[h]] + lax.dot_general(
                    v[r][:, hs[h]], k_end[:, hs[h]], (((0,), (0,)), ((), ())),
                    preferred_element_type=F32)
        o = jnp.concatenate(outs, axis=0)
        o = jnp.concatenate(
            [o[:, c] * lax.rsqrt(jnp.mean(o[:, c] * o[:, c], axis=-1, keepdims=True) + RMS_EPS)
             for c in hs], axis=1)
        g = g_ref[0, sl, :]
        o_ref[0, sl, :] = (o * ng_ref[...] * (g * _sigmoid(g))).astype(o_ref.dtype)
        return tuple(states)

    init = tuple(jnp.zeros((HG_DK, HG_DK), F32) for _ in range(heads))
    lax.fori_loop(0, lp // rows, body, init)


HG_MAX_GROUP = 17


def _hgrn2(proj3, lb, norm_g, width):
    nb, lp, _ = proj3.shape
    bw = 2 * HG_DK
    n_blk = width // bw
    n_chunks = lp // HG_CHUNK
    group = max(g for g in range(1, HG_MAX_GROUP + 1) if n_chunks % g == 0)

    def col_spec(part):
        return pl.BlockSpec((1, lp, bw), lambda b, j, part=part: (b, 0, part * n_blk + j))

    vec_spec = pl.BlockSpec((1, bw), lambda b, j: (0, j))
    return pl.pallas_call(
        functools.partial(_hgrn_kernel, chunk=HG_CHUNK, group=group),
        grid=(nb, n_blk),
        in_specs=[col_spec(0), col_spec(1), col_spec(2), col_spec(3), vec_spec, vec_spec],
        out_specs=pl.BlockSpec((1, lp, bw), lambda b, j: (b, 0, j)),
        out_shape=jax.ShapeDtypeStruct((nb, lp, width), BF16),
        compiler_params=pltpu.CompilerParams(
            dimension_semantics=("parallel", "parallel"), vmem_limit_bytes=VMEM_LIMIT),
        name="hgrn2_mixer",
    )(proj3, proj3, proj3, proj3, lb, norm_g)


def _block_tail_kernel(*refs, n_mix, alpha, ff_chunks):
    h_ref = refs[0]
    mix_refs = refs[1:1 + n_mix]
    w_refs = refs[1 + n_mix:1 + 2 * n_mix]
    (g1_ref, b1_ref, wup_ref, bup_ref, wdn_ref, bdn_ref, g2_ref, b2_ref, o_ref) = refs[1 + 2 * n_mix:]
    mix = jnp.dot(mix_refs[0][...], w_refs[0][...], preferred_element_type=F32)
    for m_ref, w_ref in zip(mix_refs[1:], w_refs[1:]):
        mix = mix + jnp.dot(m_ref[...], w_ref[...], preferred_element_type=F32)
    h1 = _layer_norm(alpha * h_ref[...] + mix, g1_ref[...], b1_ref[...])
    h1b = h1.astype(BF16)
    dff = wup_ref.shape[1]
    fc = dff // ff_chunks
    acc = alpha * h1 + bdn_ref[...]
    for c in range(ff_chunks):
        hid = jnp.dot(h1b, wup_ref[:, c * fc:(c + 1) * fc], preferred_element_type=F32)
        hid = jnp.maximum(hid + bup_ref[:, c * fc:(c + 1) * fc], 0.0)
        hid = (hid * hid).astype(BF16)
        acc = acc + jnp.dot(hid, wdn_ref[c * fc:(c + 1) * fc, :], preferred_element_type=F32)
    o_ref[...] = _layer_norm(acc, g2_ref[...], b2_ref[...])


def _block_tail(h, mixes, ws, g1, b1, wup, bup, wdn, bdn, g2, b2, alpha, window=None):
    rows, d = h.shape
    consts = [g1, b1, wup, bup, wdn, bdn, g2, b2]
    if window is None:
        tm = _row_tile(rows, ROW_TILE)
        grid = (rows // tm,)
        row_spec = lambda n: pl.BlockSpec((tm, n), lambda i: (i, 0))
        out_spec, out_rows = row_spec(d), rows
    else:
        nb, lp, start, count = window
        tm = _row_tile(count, ROW_TILE)
        per_seq = count // tm
        grid = (nb, per_seq)
        align = math.gcd(lp, start, tm)
        row_spec = lambda n: pl.BlockSpec(
            (pl.Element(tm), pl.Element(n)),
            lambda b, j: (pl.multiple_of(b * lp + start + j * tm, align), 0))
        out_spec = pl.BlockSpec((tm, d), lambda b, j: (b * per_seq + j, 0))
        out_rows = nb * count
    return pl.pallas_call(
        functools.partial(_block_tail_kernel, n_mix=len(mixes), alpha=alpha,
                          ff_chunks=FF_CHUNKS),
        grid=grid,
        in_specs=([row_spec(d)] + [row_spec(m.shape[1]) for m in mixes]
                  + [_const_spec(w.shape) for w in ws] + [_const_spec(c.shape) for c in consts]),
        out_specs=out_spec,
        out_shape=jax.ShapeDtypeStruct((out_rows, d), F32),
        compiler_params=pltpu.CompilerParams(
            dimension_semantics=("parallel",) * len(grid), vmem_limit_bytes=VMEM_LIMIT),
        name="block_tail",
    )(h, *mixes, *ws, *consts)


def kernel(x, meta, w_in_ab, s5_lam_re, s5_lam_im, s5_log_dt, s5_b_re, s5_b_im, s5_c_re, s5_c_im, s5_d, s5_w_glu, s5_b_glu, w_out_ab, w_in_c, hgrn_gamma, hgrn_norm_g, w_out_c, ln_mix_g, ln_mix_b, mlp_w_up, mlp_b_up, mlp_w_down, mlp_b_down, ln_mlp_g, ln_mlp_b):
    nb, seq, d = x.shape
    depth = ln_mix_g.shape[0]
    assert depth == 2 and w_in_ab.shape[0] == 1 and w_in_c.shape[0] == 1
    alpha = (2.0 * depth) ** 0.25
    s5w = s5_w_glu.shape[1]
    sbw = (w_in_ab.shape[2] - s5w) // 3
    hgw = w_in_c.shape[2] // 4
    ltok = N_META + seq
    lp = -(-ltok // SEQ_ALIGN) * SEQ_ALIGN
    rows = nb * lp
    row2 = lambda a: a.reshape(1, -1).astype(F32)

    h = jnp.concatenate([jnp.broadcast_to(meta.astype(x.dtype)[None], (nb, N_META, d)), x,
                         jnp.zeros((nb, lp - ltok, d), x.dtype)], axis=1).reshape(rows, d)

    u, qkv = _inproj_ab(h, w_in_ab[0].astype(BF16), s5w, sbw)
    wb, wc, a_re, a_im = _s5_params(s5_lam_re[0], s5_lam_im[0], s5_log_dt[0], s5_b_re[0],
                                    s5_b_im[0], s5_c_re[0], s5_c_im[0])
    a_out = _s5(u.reshape(nb, lp, s5w), wb, a_re, a_im, wc, row2(s5_d[0]),
                s5_w_glu[0].astype(BF16), row2(s5_b_glu[0]))
    b_out = _stick_breaking(qkv.reshape(nb, lp, 3 * sbw), sbw)
    w_out = w_out_ab[0].astype(BF16)
    h = _block_tail(h, [a_out.reshape(rows, s5w), b_out.reshape(rows, sbw)],
                    [w_out[:s5w], w_out[s5w:]],
                    row2(ln_mix_g[0]), row2(ln_mix_b[0]), mlp_w_up[0].astype(BF16),
                    row2(mlp_b_up[0]), mlp_w_down[0].astype(BF16), row2(mlp_b_down[0]),
                    row2(ln_mlp_g[0]), row2(ln_mlp_b[0]), alpha)

    proj = _inproj_c(h, w_in_c[0].astype(BF16))
    p = jax.nn.softmax(hgrn_gamma.astype(F32), axis=0)
    lb = (jnp.cumsum(p, axis=0) - p[0])[1]
    c_out = _hgrn2(proj.reshape(nb, lp, 4 * hgw), row2(lb), row2(hgrn_norm_g[0]), hgw)
    h = _block_tail(h, [c_out.reshape(rows, hgw)], [w_out_c[0].astype(BF16)],
                    row2(ln_mix_g[1]), row2(ln_mix_b[1]), mlp_w_up[1].astype(BF16),
                    row2(mlp_b_up[1]), mlp_w_down[1].astype(BF16), row2(mlp_b_down[1]),
                    row2(ln_mlp_g[1]), row2(ln_mlp_b[1]), alpha, window=(nb, lp, N_META, seq))
    return h.reshape(nb, seq, d)
```

```python
import functools
import math

import jax
import jax.numpy as jnp
from jax import lax
from jax.experimental import pallas as pl
from jax.experimental.pallas import tpu as pltpu

F32 = jnp.float32
BF16 = jnp.bfloat16

N_META = 16
S5_GROUP = 16
S5_STATE = 64
SB_HEAD_DIM = 64
HG_DK = 128
HG_CHUNK = 64
LN_EPS = 1e-5
RMS_EPS = 1e-6

LANES = 128
SEQ_ALIGN = 64
SB_TILE = 256
ROW_TILE = 1024
S5_TIME_CHUNK = SEQ_ALIGN
S5_SLAB_GROUP = 4
FF_CHUNKS = 4
VMEM_LIMIT = 56 * 1024 * 1024


def _row_tile(rows, target):
    best = 8
    for t in range(8, min(rows, target) + 1, 8):
        if rows % t == 0:
            best = t
    return best


def _const_spec(shape):
    nd = len(shape)
    return pl.BlockSpec(shape, lambda *_: (0,) * nd, pipeline_mode=pl.Buffered(1))


def _nt_dot(a, b):
    return lax.dot_general(a, b, (((1,), (1,)), ((), ())), preferred_element_type=F32)


def _split_bf16(x):
    hi = x.astype(BF16)
    lo = (x - hi.astype(F32)).astype(BF16)
    return hi, lo


def _layer_norm(x, g, b):
    mu = jnp.mean(x, axis=-1, keepdims=True)
    xc = x - mu
    var = jnp.mean(xc * xc, axis=-1, keepdims=True)
    return xc * lax.rsqrt(var + LN_EPS) * g + b


def _sigmoid(x):
    return 1.0 / (1.0 + jnp.exp(-x))


def _inproj_ab_kernel(h_ref, w_ref, u_ref, qkv_ref, *, s5w, sbw):
    h = h_ref[...].astype(BF16)
    u_ref[...] = jnp.dot(h, w_ref[:, :s5w], preferred_element_type=F32)
    q = jnp.dot(h, w_ref[:, s5w:s5w + sbw], preferred_element_type=F32)
    qkv_ref[:, :sbw] = (q * (SB_HEAD_DIM ** -0.5)).astype(BF16)
    kv = jnp.dot(h, w_ref[:, s5w + sbw:], preferred_element_type=F32)
    qkv_ref[:, sbw:] = kv.astype(BF16)


def _inproj_ab(h, w, s5w, sbw):
    rows, d = h.shape
    tm = _row_tile(rows, ROW_TILE)
    return pl.pallas_call(
        functools.partial(_inproj_ab_kernel, s5w=s5w, sbw=sbw),
        grid=(rows // tm,),
        in_specs=[pl.BlockSpec((tm, d), lambda i: (i, 0)), _const_spec(w.shape)],
        out_specs=[pl.BlockSpec((tm, s5w), lambda i: (i, 0)),
                   pl.BlockSpec((tm, 3 * sbw), lambda i: (i, 0))],
        out_shape=[jax.ShapeDtypeStruct((rows, s5w), F32),
                   jax.ShapeDtypeStruct((rows, 3 * sbw), BF16)],
        compiler_params=pltpu.CompilerParams(
            dimension_semantics=("parallel",), vmem_limit_bytes=VMEM_LIMIT),
        name="inproj_ab",
    )(h, w)


def _inproj_c_kernel(h_ref, w_ref, o_ref, *, n_split):
    h = h_ref[...].astype(BF16)
    n = w_ref.shape[1] // n_split
    for s in range(n_split):
        o_ref[:, s * n:(s + 1) * n] = jnp.dot(
            h, w_ref[:, s * n:(s + 1) * n], preferred_element_type=F32)


def _inproj_c(h, w):
    rows, d = h.shape
    n = w.shape[1]
    tm = _row_tile(rows, ROW_TILE)
    return pl.pallas_call(
        functools.partial(_inproj_c_kernel, n_split=FF_CHUNKS),
        grid=(rows // tm,),
        in_specs=[pl.BlockSpec((tm, d), lambda i: (i, 0)), _const_spec(w.shape)],
        out_specs=pl.BlockSpec((tm, n), lambda i: (i, 0)),
        out_shape=jax.ShapeDtypeStruct((rows, n), F32),
        compiler_params=pltpu.CompilerParams(
            dimension_semantics=("parallel",), vmem_limit_bytes=VMEM_LIMIT),
        name="inproj_c",
    )(h, w)


S5_SCAN_UNROLL = 2


def _s5_kernel(u_ref, perm_ref, wb_ref, are_ref, aim_ref, wc_ref, d_ref, wglu_ref, bglu_ref,
               o_ref, bu_ref, sre_ref, sim_ref, ytm_ref, y_ref, *, nb, tc, slab_group):
    width = u_ref.shape[2]
    sc = are_ref.shape[1]
    nq = width // LANES
    per_q = sc // nq

    @pl.when(pl.program_id(0) == 0)
    def _():
        sre_ref[...] = jnp.zeros_like(sre_ref)
        sim_ref[...] = jnp.zeros_like(sim_ref)

    n_slabs = sc // LANES
    slabs_q = per_q // LANES
    u = u_ref[...].reshape(nb * tc, width)
    u_tm = jnp.dot(perm_ref[...], u.astype(BF16), preferred_element_type=F32).astype(BF16)
    for q in range(nq):
        r = jnp.dot(u_tm[:, q * LANES:(q + 1) * LANES], wb_ref[q], preferred_element_type=F32)
        for j in range(slabs_q):
            bu_ref[q * slabs_q + j] = r[:, j * LANES:(j + 1) * LANES]
            bu_ref[n_slabs + q * slabs_q + j] = r[:, per_q + j * LANES:per_q + (j + 1) * LANES]

    for s0 in range(0, n_slabs, slab_group):
        slabs = list(range(s0, s0 + slab_group))
        a_re = [jnp.broadcast_to(are_ref[:, s * LANES:(s + 1) * LANES], (nb, LANES)) for s in slabs]
        a_im = [jnp.broadcast_to(aim_ref[:, s * LANES:(s + 1) * LANES], (nb, LANES)) for s in slabs]
        init = []
        for s in slabs:
            init += [sre_ref[:, s * LANES:(s + 1) * LANES], sim_ref[:, s * LANES:(s + 1) * LANES]]

        def body(t, carry, slabs=slabs, a_re=a_re, a_im=a_im):
            new = []
            rows = pl.ds(pl.multiple_of(t * nb, nb), nb)
            for i, s in enumerate(slabs):
                sr, si = carry[2 * i], carry[2 * i + 1]
                nr = a_re[i] * sr - a_im[i] * si + bu_ref[s, rows, :]
                ni = a_re[i] * si + a_im[i] * sr + bu_ref[n_slabs + s, rows, :]
                bu_ref[s, rows, :] = nr
                bu_ref[n_slabs + s, rows, :] = ni
                new += [nr, ni]
            return tuple(new)

        fin = lax.fori_loop(0, tc, body, tuple(init), unroll=S5_SCAN_UNROLL)
        for i, s in enumerate(slabs):
            sre_ref[:, s * LANES:(s + 1) * LANES] = fin[2 * i]
            sim_ref[:, s * LANES:(s + 1) * LANES] = fin[2 * i + 1]

    for q in range(nq):
        s_re = jnp.concatenate(
            [bu_ref[q * slabs_q + j].astype(BF16) for j in range(slabs_q)], axis=1)
        s_im = jnp.concatenate(
            [bu_ref[n_slabs + q * slabs_q + j].astype(BF16) for j in range(slabs_q)], axis=1)
        ytm_ref[q] = (jnp.dot(s_re, wc_ref[q, :per_q, :], preferred_element_type=F32)
                      + jnp.dot(s_im, wc_ref[q, per_q:, :], preferred_element_type=F32))
    for q in range(nq):
        for b in range(nb):
            y_ref[b * tc:(b + 1) * tc, q * LANES:(q + 1) * LANES] = (
                ytm_ref[q, pl.ds(b, tc, stride=nb), :])
    y = y_ref[...] + d_ref[...] * u
    y = 0.5 * y * (1.0 + jnp.tanh(math.sqrt(2.0 / math.pi) * (y + 0.044715 * (y * y * y))))
    gate =jnp.dot(y.astype(BF16), wglu_ref[...], preferred_element_type=F32) + bglu_ref[...]
    o_ref[...] = (y * _sigmoid(gate)).reshape(nb, tc, width).astype(o_ref.dtype)


def _s5_params(lam_re, lam_im, log_dt, b_re, b_im, c_re, c_im):
    g, p = lam_re.shape
    lr, li = lam_re.astype(F32), lam_im.astype(F32)
    dt = jnp.exp(log_dt.astype(F32))[:, None]
    mag = jnp.exp(lr * dt)
    a_re = mag * jnp.cos(li * dt)
    a_im = mag * jnp.sin(li * dt)
    den = lr * lr + li * li
    c_re_ = ((a_re - 1.0) * lr + a_im * li) / den
    c_im_ = (a_im * lr - (a_re - 1.0) * li) / den
    bb_re = c_re_[:, :, None] * b_re.astype(F32) - c_im_[:, :, None] * b_im.astype(F32)
    bb_im = c_re_[:, :, None] * b_im.astype(F32) + c_im_[:, :, None] * b_re.astype(F32)
    gq = LANES // S5_GROUP
    nq = g // gq
    eye = jnp.eye(gq, dtype=F32)

    def b_slab(x):
        x = x.reshape(nq, gq, p, S5_GROUP)
        return jnp.einsum('qgph,gk->qghkp', x, eye).reshape(nq, gq * S5_GROUP, gq * p)

    def c_slab(x):
        x = x.reshape(nq, gq, S5_GROUP, p)
        return jnp.einsum('qghp,gk->qgpkh', x, eye).reshape(nq, gq * p, gq * S5_GROUP)

    wb = jnp.concatenate([b_slab(bb_re), b_slab(bb_im)], axis=2)
    wc = jnp.concatenate([c_slab(c_re.astype(F32)), c_slab(-c_im.astype(F32))], axis=1)
    return wb.astype(BF16), wc.astype(BF16), a_re.reshape(1, g * p), a_im.reshape(1, g * p)


def _s5(u3, wb, a_re, a_im, wc, d, wglu, bglu):
    nb, lp, width = u3.shape
    sc = a_re.shape[1]
    tc = S5_TIME_CHUNK
    assert lp % tc == 0
    rows = nb * tc
    r = jnp.arange(rows)
    perm = (jnp.arange(rows)[None, :] == ((r % nb) * tc + r // nb)[:, None]).astype(BF16)
    return pl.pallas_call(
        functools.partial(_s5_kernel, nb=nb, tc=tc, slab_group=S5_SLAB_GROUP),
        grid=(lp // tc,),
        in_specs=[pl.BlockSpec((nb, tc, width), lambda t: (0, t, 0)), _const_spec(perm.shape),
                  _const_spec(wb.shape), _const_spec(a_re.shape), _const_spec(a_im.shape),
                  _const_spec(wc.shape), _const_spec(d.shape), _const_spec(wglu.shape),
                  _const_spec(bglu.shape)],
        out_specs=pl.BlockSpec((nb, tc, width), lambda t: (0, t, 0)),
        out_shape=jax.ShapeDtypeStruct((nb, lp, width), BF16),
        scratch_shapes=[pltpu.VMEM((2 * sc // LANES, rows, LANES), F32),
                        pltpu.VMEM((nb, sc), F32), pltpu.VMEM((nb, sc), F32),
                        pltpu.VMEM((width // LANES, rows, LANES), F32),
                        pltpu.VMEM((rows, width), F32)],
        compiler_params=pltpu.CompilerParams(
            dimension_semantics=("arbitrary",), vmem_limit_bytes=VMEM_LIMIT),
        name="s5_mixer",
    )(u3, perm, wb, a_re, a_im, wc, d, wglu, bglu)


MASKED_LOG = -1e30
SB_UNROLL = 4
SB_LEAD = 3
SB_TRAIL = SB_LEAD + SB_UNROLL - 1


def _sb_kernel(tbl_ref, q_ref, k_ref, v_ref, o_ref, q_scr, km_ref, vm_ref, carry_scr, acc_scr,
               tri_scr, *bufs, tq, kb, n_diag, n_off, n_tail):
    lp = k_ref.shape[1]
    n_pairs = k_ref.shape[2] // LANES
    n_q = km_ref.shape[0] // n_pairs
    n_heads = LANES // SB_HEAD_DIM
    reps = kb // LANES

    lane =lax.broadcasted_iota(jnp.int32, (kb, LANES), 1)
    for pair in range(n_pairs):
        cols = slice(pair * LANES, (pair + 1) * LANES)
        q0 = pair * n_q * tq
        q_scr[q0:q0 + lp, :] = q_ref[0, :, cols]
        if n_q * tq > lp:
            q_scr[q0 + lp:q0 + n_q * tq, :] = jnp.zeros((n_q * tq - lp, LANES), q_scr.dtype)
        for m in range(n_q):
            valid = min(kb, lp - m * kb)
            k_blk = k_ref[0, m * kb:m * kb + valid, cols]
            v_blk = v_ref[0, m * kb:m * kb + valid, cols]
            if valid < kb:
                pad = jnp.zeros((kb - valid, LANES), k_blk.dtype)
                k_blk = jnp.concatenate([k_blk, pad], axis=0)
                v_blk = jnp.concatenate([v_blk, pad], axis=0)
            for h in range(n_heads):
                in_head = (lane >= h * SB_HEAD_DIM) & (lane < (h + 1) * SB_HEAD_DIM)
                km_ref[pair * n_q + m, h] = jnp.where(in_head, k_blk, jnp.zeros_like(k_blk))
                vm_ref[pair * n_q + m, h] = jnp.where(in_head, v_blk, jnp.zeros_like(v_blk))

    row = lax.broadcasted_iota(jnp.int32, (kb, kb), 0)
    col = lax.broadcasted_iota(jnp.int32, (kb, kb), 1)
    tri = jnp.where(row > col, 1.0, 0.0).astype(BF16)
    below_diag = lax.broadcasted_iota(jnp.int32, (tq, kb), 1) < lax.broadcasted_iota(
        jnp.int32, (tq, kb), 0)

    def item(j):
        return tbl_ref[3 * j], tbl_ref[3 * j + 1], tbl_ref[3 * j + 2]

    z_buf, hl_buf, base_buf, rs_buf, w_buf = (bufs[0:2], bufs[2:4], bufs[4:6], bufs[6:8],
                                              bufs[8:10])
    tri_scr[...] = tri
    for ref in (z_buf[0], hl_buf[1], base_buf[1], rs_buf[1], w_buf[0]):
        ref[...] = jnp.zeros_like(ref)

    def scores(j, z_ref, tm):
        qt, m, _ = item(j)
        q = q_scr[pl.ds(pl.multiple_of(qt * tq, tq), tm), :]
        for h in range(n_heads):
            z_ref[h, :tm] = _nt_dot(q, km_ref[m, h])

    def log_terms(h, z_ref, hl_ref, base_ref, rs_ref, diagonal, tm):
        z = z_ref[h, :tm]
        nlk = jnp.maximum(z, 0.0) + jnp.log(1.0 + jnp.exp(-jnp.abs(z)))
        base = z - nlk
        if diagonal:
            nlk = jnp.where(below_diag, nlk, 0.0)
            base = jnp.where(below_diag, base, MASKED_LOG)
        hl_ref[h, :tm] = nlk.astype(BF16)
        base_ref[h, :tm] = base
        rs_ref[h, :tm] = jnp.broadcast_to(jnp.sum(nlk, axis=1, keepdims=True), (tm, LANES))

    def suffix_sum(h, j, hl_ref, tm, first):
        _, _, slot = item(j)
        inside = jnp.dot(hl_ref[h, :tm], tri_scr[...], preferred_element_type=F32)
        if first:
            return inside
        return inside + jnp.concatenate([carry_scr[slot, h, :tm]] * reps, axis=1)

    def weights(h, j, later, base_ref, rs_ref, w_ref, tm, first):
        _, _, slot = item(j)
        w_ref[h, :tm] = jnp.exp(base_ref[h, :tm] - later).astype(BF16)
        if first:
            carry_scr[slot, h, :tm] = rs_ref[h, :tm]
        else:
            carry_scr[slot, h, :tm] = carry_scr[slot, h, :tm] + rs_ref[h, :tm]

    def weighted_values(j, w_ref, tm, first):
        _, m, slot = item(j)
        pv = jnp.dot(w_ref[0, :tm], vm_ref[m, 0], preferred_element_type=F32)
        for h in range(1, n_heads):
            pv = pv + jnp.dot(w_ref[h, :tm], vm_ref[m, h], preferred_element_type=F32)
        acc_scr[slot, :tm] = pv if first else acc_scr[slot, :tm] + pv

    def run(first, n_real, diagonal, tm):
        n_iter = -(-(n_real + SB_LEAD) // SB_UNROLL) * SB_UNROLL

        def step(i, p):
            j = first + i
            later = []
            for h in range(n_heads):
                log_terms(h, z_buf[p], hl_buf[p], base_buf[p], rs_buf[p], diagonal, tm)
                if h == 0:
                    weighted_values(j - 2, w_buf[p], tm, diagonal)
                later.append(suffix_sum(h, j - 1, hl_buf[1 - p], tm, diagonal))
            for h in range(n_heads):
                weights(h, j - 1, later[h], base_buf[1 - p], rs_buf[1 - p], w_buf[1 - p], tm,
                        diagonal)
                if h == 0:
                    scores(j + 1, z_buf[1 - p], tm)

        def body(it, c):
            for k in range(SB_UNROLL):
                step(SB_UNROLL * it + k, k % 2)
            return c

        lax.fori_loop(0, n_iter // SB_UNROLL, body, 0)

    tail = lp - (n_q - 1) * tq
    start = 0
    for n_items, diagonal, tm in ((n_diag, True, tq), (n_off, False, tq), (n_tail, False, tail)):
        if n_items:
            run(start + SB_LEAD - 1, n_items, diagonal, tm)
        start += SB_LEAD + n_items + SB_TRAIL
    for pair in range(n_pairs):
        o_ref[0, :, pair * LANES:(pair + 1) * LANES] = acc_scr[
            pair * n_q:(pair + 1) * n_q].reshape(n_q * tq, LANES)[:lp].astype(o_ref.dtype)


def _sb_items(n_q, n_pairs, partial_last):
    dummy = [(0, 0, n_pairs * n_q)]
    base = [p * n_q for p in range(n_pairs)]
    n_full = n_q - 1 if partial_last else n_q
    diag = [(o + t, o + t, o + t) for o in base for t in range(n_q)]
    off = [(o + t, o + m, o + t) for o in base for t in range(n_full) for m in range(t - 1, -1, -1)]
    tail = [(o + t, o + m, o + t) for o in base for t in range(n_full, n_q)
            for m in range(t - 1, -1, -1)]
    flat = []
    for items in (diag, off, tail):
        flat += SB_LEAD * dummy + items + SB_TRAIL * dummy
    counts = (len(diag), len(off), len(tail))
    return counts, jnp.asarray([x for e in flat for x in e], jnp.int32)


SB_PAIRS_PER_STEP = 4


def _stick_breaking(qkv3, sbw):
    nb, lp, _ = qkv3.shape
    tq = kb = SB_TILE
    n_heads = LANES // SB_HEAD_DIM
    pps = math.gcd(SB_PAIRS_PER_STEP, sbw // LANES)
    n_steps = sbw // (pps * LANES)
    bw = pps * LANES
    n_q = -(-lp // tq)
    (n_diag, n_off, n_tail), table = _sb_items(n_q, pps, partial_last=lp % tq != 0)
    masked_kv = pltpu.VMEM((pps * n_q, n_heads, kb, LANES), BF16)
    grid_spec = pltpu.PrefetchScalarGridSpec(
        num_scalar_prefetch=1,
        grid=(nb, n_steps),
        in_specs=[pl.BlockSpec((1, lp, bw), lambda b, p, t: (b, 0, p)),
                  pl.BlockSpec((1, lp, bw), lambda b, p, t: (b, 0, n_steps + p)),
                  pl.BlockSpec((1, lp, bw), lambda b, p, t: (b, 0, 2 * n_steps + p))],
        out_specs=pl.BlockSpec((1, lp, bw), lambda b, p, t: (b, 0, p)),
        scratch_shapes=[pltpu.VMEM((pps * n_q * tq, LANES), BF16), masked_kv, masked_kv,
                        pltpu.VMEM((pps * n_q + 1, n_heads, tq, LANES), F32),
                        pltpu.VMEM((pps * n_q + 1, tq, LANES), F32),
                        pltpu.VMEM((kb, kb), BF16)]
        + 2 * [pltpu.VMEM((n_heads, tq, kb), F32)]
        + 2 * [pltpu.VMEM((n_heads, tq, kb), BF16)]
        + 2 * [pltpu.VMEM((n_heads, tq, kb), F32)]
        + 2 * [pltpu.VMEM((n_heads, tq, LANES), F32)]
        + 2 * [pltpu.VMEM((n_heads, tq, kb), BF16)])
    return pl.pallas_call(
        functools.partial(_sb_kernel, tq=tq, kb=kb, n_diag=n_diag, n_off=n_off, n_tail=n_tail),
        grid_spec=grid_spec,
        out_shape=jax.ShapeDtypeStruct((nb, lp, sbw), BF16),
        compiler_params=pltpu.CompilerParams(
            dimension_semantics=("parallel", "parallel"), vmem_limit_bytes=VMEM_LIMIT),
        name="stick_breaking",
    )(table, qkv3, qkv3, qkv3)


def _hgrn_kernel(q_ref, f_ref, v_ref, g_ref, lb_ref, ng_ref, o_ref, *, chunk, group):
    lp = q_ref.shape[1]
    rows = chunk * group
    heads = q_ref.shape[2] // HG_DK
    assert heads == 2
    row = lax.broadcasted_iota(jnp.int32, (chunk, chunk), 0)
    col = lax.broadcasted_iota(jnp.int32, (chunk, chunk), 1)
    causal = col <= row
    tri = jnp.where(causal, 1.0, 0.0).astype(BF16)
    tri2 = jnp.concatenate([tri, tri], axis=1)
    col2 = lax.broadcasted_iota(jnp.int32, (chunk, heads * chunk), 1)
    row2 = lax.broadcasted_iota(jnp.int32, (chunk, heads * chunk), 0)
    causal2 = jnp.where(col2 >= chunk, col2 - chunk, col2) <= row2
    first = lax.broadcasted_iota(jnp.int32, (chunk, heads * HG_DK), 1) < HG_DK
    zero_state = jnp.zeros((HG_DK, HG_DK), BF16)
    lb = lb_ref[...]
    hs = [slice(h * HG_DK, (h + 1) * HG_DK) for h in range(heads)]

    def block_diag(x):
        zero = jnp.zeros_like(x)
        return jnp.concatenate([jnp.where(first, x, zero), jnp.where(first, zero, x)], axis=0)

    def body(s, states):
        sl = pl.ds(pl.multiple_of(s * rows, rows), rows)
        f = lb + (1.0 - lb) * _sigmoid(f_ref[0, sl, :])
        kk = 1.0 - f
        hi, lo = _split_bf16(jnp.log(f))
        cs = [slice(c * chunk, (c + 1) * chunk) for c in range(group)]
        bcum = jnp.concatenate(
            [jnp.dot(tri2, jnp.concatenate([hi[r], lo[r]], axis=0), preferred_element_type=F32)
             for r in cs], axis=0)
        q_dec = (q_ref[0, sl, :] * jnp.exp(bcum)).astype(BF16)
        k_inv_f = kk * jnp.exp(-bcum)
        k_inv = k_inv_f.astype(BF16)
        v = v_ref[0, sl, :].astype(BF16)
        states = list(states)
        outs = []
        for r in cs:
            e_last = jnp.exp(bcum[r.stop - 1:r.stop, :])
            k_end = (k_inv_f[r] * e_last).astype(BF16)
            scores = jnp.where(causal2, _nt_dot(q_dec[r], block_diag(k_inv[r])), 0.0)
            st_bd = jnp.concatenate(
                [jnp.concatenate([states[0].astype(BF16), zero_state], axis=1),
                 jnp.concatenate([zero_state, states[1].astype(BF16)], axis=1)], axis=0)
            outs.append(jnp.dot(scores.astype(BF16), block_diag(v[r]), preferred_element_type=F32)
                        + _nt_dot(q_dec[r], st_bd))
            for h in range(heads):
                states[h] = states[h] * e_last[:, hs[h]] + lax.dot_general(
                    v[r][:, hs[h]], k_end[:, hs[h]], (((0,), (0,)), ((), ())),
                    preferred_element_type=F32)
        o = jnp.concatenate(outs, axis=0)
        o = jnp.concatenate(
            [o[:, c] * lax.rsqrt(jnp.mean(o[:, c] * o[:, c], axis=-1, keepdims=True) + RMS_EPS)
             for c in hs], axis=1)
        g = g_ref[0, sl, :]
        o_ref[0, sl, :] = (o * ng_ref[...] * (g * _sigmoid(g))).astype(o_ref.dtype)
        return tuple(states)

    init = tuple(jnp.zeros((HG_DK, HG_DK), F32) for _ in range(heads))
    lax.fori_loop(0, lp // rows, body, init)


HG_MAX_GROUP = 17


def _hgrn2(proj3, lb, norm_g, width):
    nb, lp, _ = proj3.shape
    bw = 2 * HG_DK
    n_blk = width // bw
    n_chunks = lp // HG_CHUNK
    group = max(g for g in range(1, HG_MAX_GROUP + 1) if n_chunks % g == 0)

    def col_spec(part):
        return pl.BlockSpec((1, lp, bw), lambda b, j, part=part: (b, 0, part * n_blk + j))

    vec_spec = pl.BlockSpec((1, bw), lambda b, j: (0, j))
    return pl.pallas_call(
        functools.partial(_hgrn_kernel, chunk=HG_CHUNK, group=group),
        grid=(nb, n_blk),
        in_specs=[col_spec(0), col_spec(1), col_spec(2), col_spec(3), vec_spec, vec_spec],
        out_specs=pl.BlockSpec((1, lp, bw), lambda b, j: (b, 0, j)),
        out_shape=jax.ShapeDtypeStruct((nb, lp, width), BF16),
        compiler_params=pltpu.CompilerParams(
            dimension_semantics=("parallel", "parallel"), vmem_limit_bytes=VMEM_LIMIT),
        name="hgrn2_mixer",
    )(proj3, proj3, proj3, proj3, lb, norm_g)


def _block_tail_kernel(*refs, n_mix, alpha, ff_chunks):
    h_ref = refs[0]
    mix_refs = refs[1:1 + n_mix]
    w_refs = refs[1 + n_mix:1 + 2 * n_mix]
    (g1_ref, b1_ref, wup_ref, bup_ref, wdn_ref, bdn_ref, g2_ref, b2_ref, o_ref) = refs[1 + 2 * n_mix:]
    mix = jnp.dot(mix_refs[0][...], w_refs[0][...], preferred_element_type=F32)
    for m_ref, w_ref in zip(mix_refs[1:], w_refs[1:]):
        mix = mix + jnp.dot(m_ref[...], w_ref[...], preferred_element_type=F32)
    h1 = _layer_norm(alpha * h_ref[...] + mix, g1_ref[...], b1_ref[...])
    h1b = h1.astype(BF16)
    dff = wup_ref.shape[1]
    fc = dff // ff_chunks
    acc = alpha * h1 + bdn_ref[...]
    for c in range(ff_chunks):
        hid = jnp.dot(h1b, wup_ref[:, c * fc:(c + 1) * fc], preferred_element_type=F32)
        hid = jnp.maximum(hid + bup_ref[:, c * fc:(c + 1) * fc], 0.0)
        hid = (hid * hid).astype(BF16)
        acc = acc + jnp.dot(hid, wdn_ref[c * fc:(c + 1) * fc, :], preferred_element_type=F32)
    o_ref[...] = _layer_norm(acc, g2_ref[...], b2_ref[...])


def _block_tail(h, mixes, ws, g1, b1, wup, bup, wdn, bdn, g2, b2, alpha, window=None):
    rows, d = h.shape
    consts = [g1, b1, wup, bup, wdn, bdn, g2, b2]
    if window is None:
        tm = _row_tile(rows, ROW_TILE)
        grid = (rows // tm,)
        row_spec = lambda n: pl.BlockSpec((tm, n), lambda i: (i, 0))
        out_spec, out_rows = row_spec(d), rows
    else:
        nb, lp, start, count = window
        tm = _row_tile(count, ROW_TILE)
        per_seq = count // tm
        grid = (nb, per_seq)
        align = math.gcd(lp, start, tm)
        row_spec = lambda n: pl.BlockSpec(
            (pl.Element(tm), pl.Element(n)),
            lambda b, j: (pl.multiple_of(b * lp + start + j * tm, align), 0))
        out_spec = pl.BlockSpec((tm, d), lambda b, j: (b * per_seq + j, 0))
        out_rows = nb * count
    return pl.pallas_call(
        functools.partial(_block_tail_kernel, n_mix=len(mixes), alpha=alpha,
                          ff_chunks=FF_CHUNKS),
        grid=grid,
        in_specs=([row_spec(d)] + [row_spec(m.shape[1]) for m in mixes]
                  + [_const_spec(w.shape) for w in ws] + [_const_spec(c.shape) for c in consts]),
        out_specs=out_spec,
        out_shape=jax.ShapeDtypeStruct((out_rows, d), F32),
        compiler_params=pltpu.CompilerParams(
            dimension_semantics=("parallel",) * len(grid), vmem_limit_bytes=VMEM_LIMIT),
        name="block_tail",
    )(h, *mixes, *ws, *consts)


def kernel(x, meta, w_in_ab, s5_lam_re, s5_lam_im, s5_log_dt, s5_b_re, s5_b_im, s5_c_re, s5_c_im, s5_d, s5_w_glu, s5_b_glu, w_out_ab, w_in_c, hgrn_gamma, hgrn_norm_g, w_out_c, ln_mix_g, ln_mix_b, mlp_w_up, mlp_b_up, mlp_w_down, mlp_b_down, ln_mlp_g, ln_mlp_b):
    nb, seq, d = x.shape
    depth = ln_mix_g.shape[0]
    assert depth == 2 and w_in_ab.shape[0] == 1 and w_in_c.shape[0] == 1
    alpha = (2.0 * depth) ** 0.25
    s5w = s5_w_glu.shape[1]
    sbw = (w_in_ab.shape[2] - s5w) // 3
    hgw = w_in_c.shape[2] // 4
    ltok = N_META + seq
    lp = -(-ltok // SEQ_ALIGN) * SEQ_ALIGN
    rows = nb * lp
    row2 = lambda a: a.reshape(1, -1).astype(F32)

    h = jnp.concatenate([jnp.broadcast_to(meta.astype(x.dtype)[None], (nb, N_META, d)), x,
                         jnp.zeros((nb, lp - ltok, d), x.dtype)], axis=1).reshape(rows, d)

    u, qkv = _inproj_ab(h, w_in_ab[0].astype(BF16), s5w, sbw)
    wb, wc, a_re, a_im = _s5_params(s5_lam_re[0], s5_lam_im[0], s5_log_dt[0], s5_b_re[0],
                                    s5_b_im[0], s5_c_re[0], s5_c_im[0])
    a_out = _s5(u.reshape(nb, lp, s5w), wb, a_re, a_im, wc, row2(s5_d[0]),
                s5_w_glu[0].astype(BF16), row2(s5_b_glu[0]))
    b_out = _stick_breaking(qkv.reshape(nb, lp, 3 * sbw), sbw)
    w_out = w_out_ab[0].astype(BF16)
    h = _block_tail(h, [a_out.reshape(rows, s5w), b_out.reshape(rows, sbw)],
                    [w_out[:s5w], w_out[s5w:]],
                    row2(ln_mix_g[0]), row2(ln_mix_b[0]), mlp_w_up[0].astype(BF16),
                    row2(mlp_b_up[0]), mlp_w_down[0].astype(BF16), row2(mlp_b_down[0]),
                    row2(ln_mlp_g[0]), row2(ln_mlp_b[0]), alpha)

    proj = _inproj_c(h, w_in_c[0].astype(BF16))
    p = jax.nn.softmax(hgrn_gamma.astype(F32), axis=0)
    lb = (jnp.cumsum(p, axis=0) - p[0])[1]
    c_out = _hgrn2(proj.reshape(nb, lp, 4 * hgw), row2(lb), row2(hgrn_norm_g[0]), hgw)
    h = _block_tail(h, [c_out.reshape(rows, hgw)], [w_out_c[0].astype(BF16)],
                    row2(ln_mix_g[1]), row2(ln_mix_b[1]), mlp_w_up[1].astype(BF16),
                    row2(mlp_b_up[1]), mlp_w_down[1].astype(BF16), row2(mlp_b_down[1]),
                    row2(ln_mlp_g[1]), row2(ln_mlp_b[1]), alpha, window=(nb, lp, N_META, seq))
    return h.reshape(nb, seq, d)
```

```python
import functools
import math

import jax
import jax.numpy as jnp
from jax import lax
from jax.experimental import pallas as pl
from jax.experimental.pallas import tpu as pltpu

F32 = jnp.float32
BF16 = jnp.bfloat16

N_META = 16
S5_GROUP = 16
S5_STATE = 64
SB_HEAD_DIM = 64
HG_DK = 128
HG_CHUNK = 64
LN_EPS = 1e-5
RMS_EPS = 1e-6

LANES = 128
SEQ_ALIGN = 64
SB_TILE = 256
ROW_TILE = 1024
S5_TIME_CHUNK = SEQ_ALIGN
S5_SLAB_GROUP = 4
FF_CHUNKS = 4
VMEM_LIMIT = 56 * 1024 * 1024


def _row_tile(rows, target):
    best = 8
    for t in range(8, min(rows, target) + 1, 8):
        if rows % t == 0:
            best = t
    return best


def _const_spec(shape):
    nd = len(shape)
    return pl.BlockSpec(shape, lambda *_: (0,) * nd, pipeline_mode=pl.Buffered(1))


def _nt_dot(a, b):
    return lax.dot_general(a, b, (((1,), (1,)), ((), ())), preferred_element_type=F32)


def _split_bf16(x):
    hi = x.astype(BF16)
    lo = (x - hi.astype(F32)).astype(BF16)
    return hi, lo


def _layer_norm(x, g, b):
    mu = jnp.mean(x, axis=-1, keepdims=True)
    xc = x - mu
    var = jnp.mean(xc * xc, axis=-1, keepdims=True)
    return xc * lax.rsqrt(var + LN_EPS) * g + b


def _sigmoid(x):
    return 1.0 / (1.0 + jnp.exp(-x))


def _inproj_ab_kernel(h_ref, w_ref, u_ref, qkv_ref, *, s5w, sbw):
    h = h_ref[...].astype(BF16)
    u_ref[...] = jnp.dot(h, w_ref[:, :s5w], preferred_element_type=F32)
    q = jnp.dot(h, w_ref[:, s5w:s5w + sbw], preferred_element_type=F32)
    qkv_ref[:, :sbw] = (q * (SB_HEAD_DIM ** -0.5)).astype(BF16)
    kv = jnp.dot(h, w_ref[:, s5w + sbw:], preferred_element_type=F32)
    qkv_ref[:, sbw:] = kv.astype(BF16)


def _inproj_ab(h, w, s5w, sbw):
    rows, d = h.shape
    tm = _row_tile(rows, ROW_TILE)
    return pl.pallas_call(
        functools.partial(_inproj_ab_kernel, s5w=s5w, sbw=sbw),
        grid=(rows // tm,),
        in_specs=[pl.BlockSpec((tm, d), lambda i: (i, 0)), _const_spec(w.shape)],
        out_specs=[pl.BlockSpec((tm, s5w), lambda i: (i, 0)),
                   pl.BlockSpec((tm, 3 * sbw), lambda i: (i, 0))],
        out_shape=[jax.ShapeDtypeStruct((rows, s5w), F32),
                   jax.ShapeDtypeStruct((rows, 3 * sbw), BF16)],
        compiler_params=pltpu.CompilerParams(
            dimension_semantics=("parallel",), vmem_limit_bytes=VMEM_LIMIT),
        name="inproj_ab",
    )(h, w)


def _inproj_c_kernel(h_ref, w_ref, o_ref, *, n_split):
    h = h_ref[...].astype(BF16)
    n = w_ref.shape[1] // n_split
    for s in range(n_split):
        o_ref[:, s * n:(s + 1) * n] = jnp.dot(
            h, w_ref[:, s * n:(s + 1) * n], preferred_element_type=F32)


def _inproj_c(h, w):
    rows, d = h.shape
    n = w.shape[1]
    tm = _row_tile(rows, ROW_TILE)
    return pl.pallas_call(
        functools.partial(_inproj_c_kernel, n_split=FF_CHUNKS),
        grid=(rows // tm,),
        in_specs=[pl.BlockSpec((tm, d), lambda i: (i, 0)), _const_spec(w.shape)],
        out_specs=pl.BlockSpec((tm, n), lambda i: (i, 0)),
        out_shape=jax.ShapeDtypeStruct((rows, n), F32),
        compiler_params=pltpu.CompilerParams(
            dimension_semantics=("parallel",), vmem_limit_bytes=VMEM_LIMIT),
        name="inproj_c",
    )(h, w)


S5_SCAN_UNROLL = 2


def _s5_kernel(u_ref, perm_ref, wb_ref, are_ref, aim_ref, wc_ref, d_ref, wglu_ref, bglu_ref,
               o_ref, bu_ref, sre_ref, sim_ref, ytm_ref, y_ref, *, nb, tc, slab_group):
    width = u_ref.shape[2]
    sc = are_ref.shape[1]
    nq = width // LANES
    per_q = sc // nq

    @pl.when(pl.program_id(0) == 0)
    def _():
        sre_ref[...] = jnp.zeros_like(sre_ref)
        sim_ref[...] = jnp.zeros_like(sim_ref)

    n_slabs = sc // LANES
    slabs_q = per_q // LANES
    u = u_ref[...].reshape(nb * tc, width)
    u_tm = jnp.dot(perm_ref[...], u.astype(BF16), preferred_element_type=F32).astype(BF16)
    for q in range(nq):
        r = jnp.dot(u_tm[:, q * LANES:(q + 1) * LANES], wb_ref[q], preferred_element_type=F32)
        for j in range(slabs_q):
            bu_ref[q * slabs_q + j] = r[:, j * LANES:(j + 1) * LANES]
            bu_ref[n_slabs + q * slabs_q + j] = r[:, per_q + j * LANES:per_q + (j + 1) * LANES]

    for s0 in range(0, n_slabs, slab_group):
        slabs = list(range(s0, s0 + slab_group))
        a_re = [jnp.broadcast_to(are_ref[:, s * LANES:(s + 1) * LANES], (nb, LANES)) for s in slabs]
        a_im = [jnp.broadcast_to(aim_ref[:, s * LANES:(s + 1) * LANES], (nb, LANES)) for s in slabs]
        init = []
        for s in slabs:
            init += [sre_ref[:, s * LANES:(s + 1) * LANES], sim_ref[:, s * LANES:(s + 1) * LANES]]

        def body(t, carry, slabs=slabs, a_re=a_re, a_im=a_im):
            new = []
            rows = pl.ds(pl.multiple_of(t * nb, nb), nb)
            for i, s in enumerate(slabs):
                sr, si = carry[2 * i], carry[2 * i + 1]
                nr = a_re[i] * sr - a_im[i] * si + bu_ref[s, rows, :]
                ni = a_re[i] * si + a_im[i] * sr + bu_ref[n_slabs + s, rows, :]
                bu_ref[s, rows, :] = nr
                bu_ref[n_slabs + s, rows, :] = ni
                new += [nr, ni]
            return tuple(new)

        fin = lax.fori_loop(0, tc, body, tuple(init), unroll=S5_SCAN_UNROLL)
        for i, s in enumerate(slabs):
            sre_ref[:, s * LANES:(s + 1) * LANES] = fin[2 * i]
            sim_ref[:, s * LANES:(s + 1) * LANES] = fin[2 * i + 1]

    for q in range(nq):
        s_re = jnp.concatenate(
            [bu_ref[q * slabs_q + j].astype(BF16) for j in range(slabs_q)], axis=1)
        s_im = jnp.concatenate(
            [bu_ref[n_slabs + q * slabs_q + j].astype(BF16) for j in range(slabs_q)], axis=1)
        ytm_ref[q] = (jnp.dot(s_re, wc_ref[q, :per_q, :], preferred_element_type=F32)
                      + jnp.dot(s_im, wc_ref[q, per_q:, :], preferred_element_type=F32))
    for q in range(nq):
        for b in range(nb):
            y_ref[b * tc:(b + 1) * tc, q * LANES:(q + 1) * LANES] = (
                ytm_ref[q, pl.ds(b, tc, stride=nb), :])
    y = y_ref[...] + d_ref[...] * u
    y = 0.5 * y * (1.0 + jnp.tanh(math.sqrt(2.0 / math.pi) * (y + 0.044715 * (y * y * y))))
    gate =jnp.dot(y.astype(BF16), wglu_ref[...], preferred_element_type=F32) + bglu_ref[...]
    o_ref[...] = (y * _sigmoid(gate)).reshape(nb, tc, width).astype(o_ref.dtype)


def _s5_params(lam_re, lam_im, log_dt, b_re, b_im, c_re, c_im):
    g, p = lam_re.shape
    lr, li = lam_re.astype(F32), lam_im.astype(F32)
    dt = jnp.exp(log_dt.astype(F32))[:, None]
    mag = jnp.exp(lr * dt)
    a_re = mag * jnp.cos(li * dt)
    a_im = mag * jnp.sin(li * dt)
    den = lr * lr + li * li
    c_re_ = ((a_re - 1.0) * lr + a_im * li) / den
    c_im_ = (a_im * lr - (a_re - 1.0) * li) / den
    bb_re = c_re_[:, :, None] * b_re.astype(F32) - c_im_[:, :, None] * b_im.astype(F32)
    bb_im = c_re_[:, :, None] * b_im.astype(F32) + c_im_[:, :, None] * b_re.astype(F32)
    gq = LANES // S5_GROUP
    nq = g // gq
    eye = jnp.eye(gq, dtype=F32)

    def b_slab(x):
        x = x.reshape(nq, gq, p, S5_GROUP)
        return jnp.einsum('qgph,gk->qghkp', x, eye).reshape(nq, gq * S5_GROUP, gq * p)

    def c_slab(x):
        x = x.reshape(nq, gq, S5_GROUP, p)
        return jnp.einsum('qghp,gk->qgpkh', x, eye).reshape(nq, gq * p, gq * S5_GROUP)

    wb = jnp.concatenate([b_slab(bb_re), b_slab(bb_im)], axis=2)
    wc = jnp.concatenate([c_slab(c_re.astype(F32)), c_slab(-c_im.astype(F32))], axis=1)
    return wb.astype(BF16), wc.astype(BF16), a_re.reshape(1, g * p), a_im.reshape(1, g * p)


def _s5(u3, wb, a_re, a_im, wc, d, wglu, bglu):
    nb, lp, width = u3.shape
    sc = a_re.shape[1]
    tc = S5_TIME_CHUNK
    assert lp % tc == 0
    rows = nb * tc
    r = jnp.arange(rows)
    perm = (jnp.arange(rows)[None, :] == ((r % nb) * tc + r // nb)[:, None]).astype(BF16)
    return pl.pallas_call(
        functools.partial(_s5_kernel, nb=nb, tc=tc, slab_group=S5_SLAB_GROUP),
        grid=(lp // tc,),
        in_specs=[pl.BlockSpec((nb, tc, width), lambda t: (0, t, 0)), _const_spec(perm.shape),
                  _const_spec(wb.shape), _const_spec(a_re.shape), _const_spec(a_im.shape),
                  _const_spec(wc.shape), _const_spec(d.shape), _const_spec(wglu.shape),
                  _const_spec(bglu.shape)],
        out_specs=pl.BlockSpec((nb, tc, width), lambda t: (0, t, 0)),
        out_shape=jax.ShapeDtypeStruct((nb, lp, width), BF16),
        scratch_shapes=[pltpu.VMEM((2 * sc // LANES, rows, LANES), F32),
                        pltpu.VMEM((nb, sc), F32), pltpu.VMEM((nb, sc), F32),
                        pltpu.VMEM((width // LANES, rows, LANES), F32),
                        pltpu.VMEM((rows, width), F32)],
        compiler_params=pltpu.CompilerParams(
            dimension_semantics=("arbitrary",), vmem_limit_bytes=VMEM_LIMIT),
        name="s5_mixer",
    )(u3, perm, wb, a_re, a_im, wc, d, wglu, bglu)


MASKED_LOG = -1e30
SB_UNROLL = 4
SB_LEAD = 3
SB_TRAIL = SB_LEAD + SB_UNROLL - 1


def _sb_kernel(tbl_ref, q_ref, k_ref, v_ref, o_ref, q_scr, km_ref, vm_ref, carry_scr, acc_scr,
               tri_scr, *bufs, tq, kb, n_diag, n_off, n_tail):
    lp = k_ref.shape[1]
    n_pairs = k_ref.shape[2] // LANES
    n_q = km_ref.shape[0] // n_pairs
    n_heads = LANES // SB_HEAD_DIM
    reps = kb // LANES

    lane =lax.broadcasted_iota(jnp.int32, (kb, LANES), 1)
    for pair in range(n_pairs):
        cols = slice(pair * LANES, (pair + 1) * LANES)
        q0 = pair * n_q * tq
        q_scr[q0:q0 + lp, :] = q_ref[0, :, cols]
        if n_q * tq > lp:
            q_scr[q0 + lp:q0 + n_q * tq, :] = jnp.zeros((n_q * tq - lp, LANES), q_scr.dtype)
        for m in range(n_q):
            valid = min(kb, lp - m * kb)
            k_blk = k_ref[0, m * kb:m * kb + valid, cols]
            v_blk = v_ref[0, m * kb:m * kb + valid, cols]
            if valid < kb:
                pad = jnp.zeros((kb - valid, LANES), k_blk.dtype)
                k_blk = jnp.concatenate([k_blk, pad], axis=0)
                v_blk = jnp.concatenate([v_blk, pad], axis=0)
            for h in range(n_heads):
                in_head = (lane >= h * SB_HEAD_DIM) & (lane < (h + 1) * SB_HEAD_DIM)
                km_ref[pair * n_q + m, h] = jnp.where(in_head, k_blk, jnp.zeros_like(k_blk))
                vm_ref[pair * n_q + m, h] = jnp.where(in_head, v_blk, jnp.zeros_like(v_blk))

    row = lax.broadcasted_iota(jnp.int32, (kb, kb), 0)
    col = lax.broadcasted_iota(jnp.int32, (kb, kb), 1)
    tri = jnp.where(row > col, 1.0, 0.0).astype(BF16)
    below_diag = lax.broadcasted_iota(jnp.int32, (tq, kb), 1) < lax.broadcasted_iota(
        jnp.int32, (tq, kb), 0)

    def item(j):
        return tbl_ref[3 * j], tbl_ref[3 * j + 1], tbl_ref[3 * j + 2]

    z_buf, hl_buf, base_buf, rs_buf, w_buf = (bufs[0:2], bufs[2:4], bufs[4:6], bufs[6:8],
                                              bufs[8:10])
    tri_scr[...] = tri
    for ref in (z_buf[0], hl_buf[1], base_buf[1], rs_buf[1], w_buf[0]):
        ref[...] = jnp.zeros_like(ref)

    def scores(j, z_ref, tm):
        qt, m, _ = item(j)
        q = q_scr[pl.ds(pl.multiple_of(qt * tq, tq), tm), :]
        for h in range(n_heads):
            z_ref[h, :tm] = _nt_dot(q, km_ref[m, h])

    def log_terms(h, z_ref, hl_ref, base_ref, rs_ref, diagonal, tm):
        z = z_ref[h, :tm]
        nlk = jnp.maximum(z, 0.0) + jnp.log(1.0 + jnp.exp(-jnp.abs(z)))
        base = z - nlk
        if diagonal:
            nlk = jnp.where(below_diag, nlk, 0.0)
            base = jnp.where(below_diag, base, MASKED_LOG)
        hl_ref[h, :tm] = nlk.astype(BF16)
        base_ref[h, :tm] = base
        rs_ref[h, :tm] = jnp.broadcast_to(jnp.sum(nlk, axis=1, keepdims=True), (tm, LANES))

    def suffix_sums(j, hl_ref, tm, first):
        _, _, slot = item(j)
        stacked = jnp.concatenate([hl_ref[h, :tm] for h in range(n_heads)], axis=0)
        inside = jnp.dot(stacked, tri_scr[...], preferred_element_type=F32)
        parts = [inside[h * tm:(h + 1) * tm] for h in range(n_heads)]
        if first:
            return parts
        return [parts[h] + jnp.concatenate([carry_scr[slot, h, :tm]] * reps, axis=1)
                for h in range(n_heads)]

    def weights(h, j, later, base_ref, rs_ref, w_ref, tm, first):
        _, _, slot = item(j)
        w_ref[h, :tm] = jnp.exp(base_ref[h, :tm] - later).astype(BF16)
        if first:
            carry_scr[slot, h, :tm] = rs_ref[h, :tm]
        else:
            carry_scr[slot, h, :tm] = carry_scr[slot, h, :tm] + rs_ref[h, :tm]

    def weighted_values(j, w_ref, tm, first):
        _, m, slot = item(j)
        pv = jnp.dot(w_ref[0, :tm], vm_ref[m, 0], preferred_element_type=F32)
        for h in range(1, n_heads):
            pv = pv + jnp.dot(w_ref[h, :tm], vm_ref[m, h], preferred_element_type=F32)
        acc_scr[slot, :tm] = pv if first else acc_scr[slot, :tm] + pv

    def run(first, n_real, diagonal, tm):
        n_iter = -(-(n_real + SB_LEAD) // SB_UNROLL) * SB_UNROLL

        def step(i, p):
            j = first + i
            later = suffix_sums(j - 1, hl_buf[1 - p], tm, diagonal)
            for h in range(n_heads):
                log_terms(h, z_buf[p], hl_buf[p], base_buf[p], rs_buf[p], diagonal, tm)
                if h == 0:
                    weighted_values(j - 2, w_buf[p], tm, diagonal)
            for h in range(n_heads):
                weights(h, j - 1, later[h], base_buf[1 - p], rs_buf[1 - p], w_buf[1 - p], tm,
                        diagonal)
                if h == 0:
                    scores(j + 1, z_buf[1 - p], tm)

        def body(it, c):
            for k in range(SB_UNROLL):
                step(SB_UNROLL * it + k, k % 2)
            return c

        lax.fori_loop(0, n_iter // SB_UNROLL, body, 0)

    tail = lp - (n_q - 1) * tq
    start = 0
    for n_items, diagonal, tm in ((n_diag, True, tq), (n_off, False, tq), (n_tail, False, tail)):
        if n_items:
            run(start + SB_LEAD - 1, n_items, diagonal, tm)
        start += SB_LEAD + n_items + SB_TRAIL
    for pair in range(n_pairs):
        o_ref[0, :, pair * LANES:(pair + 1) * LANES] = acc_scr[
            pair * n_q:(pair + 1) * n_q].reshape(n_q * tq, LANES)[:lp].astype(o_ref.dtype)


def _sb_items(n_q, n_pairs, partial_last):
    dummy = [(0, 0, n_pairs * n_q)]
    base = [p * n_q for p in range(n_pairs)]
    n_full = n_q - 1 if partial_last else n_q
    diag = [(o + t, o + t, o + t) for o in base for t in range(n_q)]
    off = [(o + t, o + m, o + t) for o in base for t in range(n_full) for m in range(t - 1, -1, -1)]
    tail = [(o + t, o + m, o + t) for o in base for t in range(n_full, n_q)
            for m in range(t - 1, -1, -1)]
    flat = []
    for items in (diag, off, tail):
        flat += SB_LEAD * dummy + items + SB_TRAIL * dummy
    counts = (len(diag), len(off), len(tail))
    return counts, jnp.asarray([x for e in flat for x in e], jnp.int32)


SB_PAIRS_PER_STEP = 4


def _stick_breaking(qkv3, sbw):
    nb, lp, _ = qkv3.shape
    tq = kb = SB_TILE
    n_heads = LANES // SB_HEAD_DIM
    pps = math.gcd(SB_PAIRS_PER_STEP, sbw // LANES)
    n_steps = sbw // (pps * LANES)
    bw = pps * LANES
    n_q = -(-lp // tq)
    (n_diag, n_off, n_tail), table = _sb_items(n_q, pps, partial_last=lp % tq != 0)
    masked_kv = pltpu.VMEM((pps * n_q, n_heads, kb, LANES), BF16)
    grid_spec = pltpu.PrefetchScalarGridSpec(
        num_scalar_prefetch=1,
        grid=(nb, n_steps),
        in_specs=[pl.BlockSpec((1, lp, bw), lambda b, p, t: (b, 0, p)),
                  pl.BlockSpec((1, lp, bw), lambda b, p, t: (b, 0, n_steps + p)),
                  pl.BlockSpec((1, lp, bw), lambda b, p, t: (b, 0, 2 * n_steps + p))],
        out_specs=pl.BlockSpec((1, lp, bw), lambda b, p, t: (b, 0, p)),
        scratch_shapes=[pltpu.VMEM((pps * n_q * tq, LANES), BF16), masked_kv, masked_kv,
                        pltpu.VMEM((pps * n_q + 1, n_heads, tq, LANES), F32),
                        pltpu.VMEM((pps * n_q + 1, tq, LANES), F32),
                        pltpu.VMEM((kb, kb), BF16)]
        + 2 * [pltpu.VMEM((n_heads, tq, kb), F32)]
        + 2 * [pltpu.VMEM((n_heads, tq, kb), BF16)]
        + 2 * [pltpu.VMEM((n_heads, tq, kb), F32)]
        + 2 * [pltpu.VMEM((n_heads, tq, LANES), F32)]
        + 2 * [pltpu.VMEM((n_heads, tq, kb), BF16)])
    return pl.pallas_call(
        functools.partial(_sb_kernel, tq=tq, kb=kb, n_diag=n_diag, n_off=n_off, n_tail=n_tail),
        grid_spec=grid_spec,
        out_shape=jax.ShapeDtypeStruct((nb, lp, sbw), BF16),
        compiler_params=pltpu.CompilerParams(
            dimension_semantics=("parallel", "parallel"), vmem_limit_bytes=VMEM_LIMIT),
        name="stick_breaking",
    )(table, qkv3, qkv3, qkv3)


def _hgrn_kernel(q_ref, f_ref, v_ref, g_ref, lb_ref, ng_ref, o_ref, *, chunk, group):
    lp = q_ref.shape[1]
    rows = chunk * group
    heads = q_ref.shape[2] // HG_DK
    assert heads == 2
    row = lax.broadcasted_iota(jnp.int32, (chunk, chunk), 0)
    col = lax.broadcasted_iota(jnp.int32, (chunk, chunk), 1)
    causal = col <= row
    tri = jnp.where(causal, 1.0, 0.0).astype(BF16)
    tri2 = jnp.concatenate([tri, tri], axis=1)
    col2 = lax.broadcasted_iota(jnp.int32, (chunk, heads * chunk), 1)
    row2 = lax.broadcasted_iota(jnp.int32, (chunk, heads * chunk), 0)
    causal2 = jnp.where(col2 >= chunk, col2 - chunk, col2) <= row2
    first = lax.broadcasted_iota(jnp.int32, (chunk, heads * HG_DK), 1) < HG_DK
    zero_state = jnp.zeros((HG_DK, HG_DK), BF16)
    lb = lb_ref[...]
    hs = [slice(h * HG_DK, (h + 1) * HG_DK) for h in range(heads)]

    def block_diag(x):
        zero = jnp.zeros_like(x)
        return jnp.concatenate([jnp.where(first, x, zero), jnp.where(first, zero, x)], axis=0)

    def body(s, states):
        sl = pl.ds(pl.multiple_of(s * rows, rows), rows)
        f = lb + (1.0 - lb) * _sigmoid(f_ref[0, sl, :])
        kk = 1.0 - f
        hi, lo = _split_bf16(jnp.log(f))
        cs = [slice(c * chunk, (c + 1) * chunk) for c in range(group)]
        bcum = jnp.concatenate(
            [jnp.dot(tri2, jnp.concatenate([hi[r], lo[r]], axis=0), preferred_element_type=F32)
             for r in cs], axis=0)
        q_dec = (q_ref[0, sl, :] * jnp.exp(bcum)).astype(BF16)
        k_inv_f = kk * jnp.exp(-bcum)
        k_inv = k_inv_f.astype(BF16)
        v = v_ref[0, sl, :].astype(BF16)
        states = list(states)
        outs = []
        for r in cs:
            e_last = jnp.exp(bcum[r.stop - 1:r.stop, :])
            k_end = (k_inv_f[r] * e_last).astype(BF16)
            scores = jnp.where(causal2, _nt_dot(q_dec[r], block_diag(k_inv[r])), 0.0)
            st_bd = jnp.concatenate(
                [jnp.concatenate([states[0].astype(BF16), zero_state], axis=1),
                 jnp.concatenate([zero_state, states[1].astype(BF16)], axis=1)], axis=0)
            outs.append(jnp.dot(scores.astype(BF16), block_diag(v[r]), preferred_element_type=F32)
                        + _nt_dot(q_dec[r], st_bd))
            for h in range(heads):
                states[h] = states[h] * e_last[:, hs[h]] + lax.dot_general(
                    v[r][:, hs[h]], k_end[:, hs[h]], (((0,), (0,)), ((), ())),
                    preferred_element_type=F32)
        o = jnp.concatenate(outs, axis=0)
        o = jnp.concatenate(
            [o[:, c] * lax.rsqrt(jnp.mean(o[:, c] * o[:, c], axis=-1, keepdims=True) + RMS_EPS)
             for c in hs], axis=1)
        g = g_ref[0, sl, :]
        o_ref[0, sl, :] = (o * ng_ref[...] * (g * _sigmoid(g))).astype(o_ref.dtype)
        return tuple(states)

    init = tuple(jnp.zeros((HG_DK, HG_DK), F32) for _ in range(heads))
    lax.fori_loop(0, lp // rows, body, init)


HG_MAX_GROUP = 17


def _hgrn2(proj3, lb, norm_g, width):
    nb, lp, _ = proj3.shape
    bw = 2 * HG_DK
    n_blk = width // bw
    n_chunks = lp // HG_CHUNK
    group = max(g for g in range(1, HG_MAX_GROUP + 1) if n_chunks % g == 0)

    def col_spec(part):
        return pl.BlockSpec((1, lp, bw), lambda b, j, part=part: (b, 0, part * n_blk + j))

    vec_spec = pl.BlockSpec((1, bw), lambda b, j: (0, j))
    return pl.pallas_call(
        functools.partial(_hgrn_kernel, chunk=HG_CHUNK, group=group),
        grid=(nb, n_blk),
        in_specs=[col_spec(0), col_spec(1), col_spec(2), col_spec(3), vec_spec, vec_spec],
        out_specs=pl.BlockSpec((1, lp, bw), lambda b, j: (b, 0, j)),
        out_shape=jax.ShapeDtypeStruct((nb, lp, width), BF16),
        compiler_params=pltpu.CompilerParams(
            dimension_semantics=("parallel", "parallel"), vmem_limit_bytes=VMEM_LIMIT),
        name="hgrn2_mixer",
    )(proj3, proj3, proj3, proj3, lb, norm_g)


def _block_tail_kernel(*refs, n_mix, alpha, ff_chunks):
    h_ref = refs[0]
    mix_refs = refs[1:1 + n_mix]
    w_refs = refs[1 + n_mix:1 + 2 * n_mix]
    (g1_ref, b1_ref, wup_ref, bup_ref, wdn_ref, bdn_ref, g2_ref, b2_ref, o_ref) = refs[1 + 2 * n_mix:]
    mix = jnp.dot(mix_refs[0][...], w_refs[0][...], preferred_element_type=F32)
    for m_ref, w_ref in zip(mix_refs[1:], w_refs[1:]):
        mix = mix + jnp.dot(m_ref[...], w_ref[...], preferred_element_type=F32)
    h1 = _layer_norm(alpha * h_ref[...] + mix, g1_ref[...], b1_ref[...])
    h1b = h1.astype(BF16)
    dff = wup_ref.shape[1]
    fc = dff // ff_chunks
    acc = alpha * h1 + bdn_ref[...]
    for c in range(ff_chunks):
        hid = jnp.dot(h1b, wup_ref[:, c * fc:(c + 1) * fc], preferred_element_type=F32)
        hid = jnp.maximum(hid + bup_ref[:, c * fc:(c + 1) * fc], 0.0)
        hid = (hid * hid).astype(BF16)
        acc = acc + jnp.dot(hid, wdn_ref[c * fc:(c + 1) * fc, :], preferred_element_type=F32)
    o_ref[...] = _layer_norm(acc, g2_ref[...], b2_ref[...])


def _block_tail(h, mixes, ws, g1, b1, wup, bup, wdn, bdn, g2, b2, alpha, window=None):
    rows, d = h.shape
    consts = [g1, b1, wup, bup, wdn, bdn, g2, b2]
    if window is None:
        tm = _row_tile(rows, ROW_TILE)
        grid = (rows // tm,)
        row_spec = lambda n: pl.BlockSpec((tm, n), lambda i: (i, 0))
        out_spec, out_rows = row_spec(d), rows
    else:
        nb, lp, start, count = window
        tm = _row_tile(count, ROW_TILE)
        per_seq = count // tm
        grid = (nb, per_seq)
        align = math.gcd(lp, start, tm)
        row_spec = lambda n: pl.BlockSpec(
            (pl.Element(tm), pl.Element(n)),
            lambda b, j: (pl.multiple_of(b * lp + start + j * tm, align), 0))
        out_spec = pl.BlockSpec((tm, d), lambda b, j: (b * per_seq + j, 0))
        out_rows = nb * count
    return pl.pallas_call(
        functools.partial(_block_tail_kernel, n_mix=len(mixes), alpha=alpha,
                          ff_chunks=FF_CHUNKS),
        grid=grid,
        in_specs=([row_spec(d)] + [row_spec(m.shape[1]) for m in mixes]
                  + [_const_spec(w.shape) for w in ws] + [_const_spec(c.shape) for c in consts]),
        out_specs=out_spec,
        out_shape=jax.ShapeDtypeStruct((out_rows, d), F32),
        compiler_params=pltpu.CompilerParams(
            dimension_semantics=("parallel",) * len(grid), vmem_limit_bytes=VMEM_LIMIT),
        name="block_tail",
    )(h, *mixes, *ws, *consts)


def kernel(x, meta, w_in_ab, s5_lam_re, s5_lam_im, s5_log_dt, s5_b_re, s5_b_im, s5_c_re, s5_c_im, s5_d, s5_w_glu, s5_b_glu, w_out_ab, w_in_c, hgrn_gamma, hgrn_norm_g, w_out_c, ln_mix_g, ln_mix_b, mlp_w_up, mlp_b_up, mlp_w_down, mlp_b_down, ln_mlp_g, ln_mlp_b):
    nb, seq, d = x.shape
    depth = ln_mix_g.shape[0]
    assert depth == 2 and w_in_ab.shape[0] == 1 and w_in_c.shape[0] == 1
    alpha = (2.0 * depth) ** 0.25
    s5w = s5_w_glu.shape[1]
    sbw = (w_in_ab.shape[2] - s5w) // 3
    hgw = w_in_c.shape[2] // 4
    ltok = N_META + seq
    lp = -(-ltok // SEQ_ALIGN) * SEQ_ALIGN
    rows = nb * lp
    row2 = lambda a: a.reshape(1, -1).astype(F32)

    h = jnp.concatenate([jnp.broadcast_to(meta.astype(x.dtype)[None], (nb, N_META, d)), x,
                         jnp.zeros((nb, lp - ltok, d), x.dtype)], axis=1).reshape(rows, d)

    u, qkv = _inproj_ab(h, w_in_ab[0].astype(BF16), s5w, sbw)
    wb, wc, a_re, a_im = _s5_params(s5_lam_re[0], s5_lam_im[0], s5_log_dt[0], s5_b_re[0],
                                    s5_b_im[0], s5_c_re[0], s5_c_im[0])
    a_out = _s5(u.reshape(nb, lp, s5w), wb, a_re, a_im, wc, row2(s5_d[0]),
                s5_w_glu[0].astype(BF16), row2(s5_b_glu[0]))
    b_out = _stick_breaking(qkv.reshape(nb, lp, 3 * sbw), sbw)
    w_out = w_out_ab[0].astype(BF16)
    h = _block_tail(h, [a_out.reshape(rows, s5w), b_out.reshape(rows, sbw)],
                    [w_out[:s5w], w_out[s5w:]],
                    row2(ln_mix_g[0]), row2(ln_mix_b[0]), mlp_w_up[0].astype(BF16),
                    row2(mlp_b_up[0]), mlp_w_down[0].astype(BF16), row2(mlp_b_down[0]),
                    row2(ln_mlp_g[0]), row2(ln_mlp_b[0]), alpha)

    proj = _inproj_c(h, w_in_c[0].astype(BF16))
    p = jax.nn.softmax(hgrn_gamma.astype(F32), axis=0)
    lb = (jnp.cumsum(p, axis=0) - p[0])[1]
    c_out = _hgrn2(proj.reshape(nb, lp, 4 * hgw), row2(lb), row2(hgrn_norm_g[0]), hgw)
    h = _block_tail(h, [c_out.reshape(rows, hgw)], [w_out_c[0].astype(BF16)],
                    row2(ln_mix_g[1]), row2(ln_mix_b[1]), mlp_w_up[1].astype(BF16),
                    row2(mlp_b_up[1]), mlp_w_down[1].astype(BF16), row2(mlp_b_down[1]),
                    row2(ln_mlp_g[1]), row2(ln_mlp_b[1]), alpha, window=(nb, lp, N_META, seq))
    return h.reshape(nb, seq, d)
```
